```python
import math
import jax, jax.numpy as jnp
from jax import lax
import numpy as np

D_MODEL = 2048
BATCH = 1
SEQ = 8192
DEPTH = 4

N_MIXERS = 2
N_HEADS = 16
HEAD_DIM = 128
BRANCH = N_HEADS * HEAD_DIM
Q_RANK = 512
KV_RANK = 256
IDX_HEADS = 16
IDX_DIM = 128
TOPK_MAX = 256
DSA_QBLOCK = 128
MOBA_BLOCK = 256
MOBA_TOPK = 3
MOBA_QCHUNK = 32
LN_EPS = 1e-5
DN_ALPHA = float((2 * DEPTH) ** 0.25)
DN_BETA = float((8 * DEPTH) ** -0.25)
N_DSA = (DEPTH + 1) // 2
N_MOBA = DEPTH // 2
DSA_IN = Q_RANK + KV_RANK + IDX_DIM + IDX_HEADS + BRANCH
DSA_SPLITS = (Q_RANK, Q_RANK + KV_RANK, Q_RANK + KV_RANK + IDX_DIM, Q_RANK + KV_RANK + IDX_DIM + IDX_HEADS)
MOBA_IN = 4 * BRANCH

kernel_name = "hybrid_dsa_moba_deepnorm_adaln"


def alibi_slopes():
    return jnp.asarray(2.0 ** (-8.0 * np.arange(1, N_HEADS + 1) / N_HEADS), dtype=jnp.float32)


def layer_norm(x, g, b):
    xf = x.astype(jnp.float32)
    mu = jnp.mean(xf, axis=-1, keepdims=True)
    var = jnp.mean(jnp.square(xf - mu), axis=-1, keepdims=True)
    y = (xf - mu) * lax.rsqrt(var + LN_EPS)
    return (y * g.astype(jnp.float32) + b.astype(jnp.float32)).astype(x.dtype)


def rms_norm(x, g):
    xf = x.astype(jnp.float32)
    y = xf * lax.rsqrt(jnp.mean(jnp.square(xf), axis=-1, keepdims=True) + LN_EPS)
    return (y * g.astype(jnp.float32)).astype(x.dtype)


def dsa_mixer(h, w_in, g_q, g_kv, w_uq, w_qi, w_uk, w_uv, w_o):
    B, S, _ = h.shape
    proj = h @ w_in
    c_q, c_kv, k_idx, w_idx, gate = jnp.split(proj, DSA_SPLITS, axis=-1)
    c_q = rms_norm(c_q, g_q)
    c_kv = rms_norm(c_kv, g_kv)
    q = (c_q @ w_uq).reshape(B, S, N_HEADS, HEAD_DIM)
    q_lat = jnp.einsum('bshd,hdc->bshc', q, w_uk)
    q_idx = (c_q @ w_qi).reshape(B, S, IDX_HEADS, IDX_DIM).astype(jnp.float32)
    k_idx = k_idx.astype(jnp.float32)
    w_idx = w_idx.astype(jnp.float32) * (IDX_HEADS ** -0.5)
    topk = min(TOPK_MAX, S // 4)
    slopes = alibi_slopes()
    scale = HEAD_DIM ** -0.5
    s_pos = jnp.arange(S, dtype=jnp.int32)
    b_idx = jnp.arange(B)[:, None, None]

    def block(qb):
        start = qb * DSA_QBLOCK
        qi = lax.dynamic_slice_in_dim(q_idx, start, DSA_QBLOCK, axis=1)
        wi = lax.dynamic_slice_in_dim(w_idx, start, DSA_QBLOCK, axis=1)
        ql = lax.dynamic_slice_in_dim(q_lat, start, DSA_QBLOCK, axis=1)
        t = start + jnp.arange(DSA_QBLOCK, dtype=jnp.int32)
        logits = jnp.einsum('bqhd,bsd->bhqs', qi, k_idx) * (IDX_DIM ** -0.5)
        score = jnp.einsum('bhqs,bqh->bqs', jax.nn.relu(logits), wi)
        causal = s_pos[None, :] <= t[:, None]
        score = jnp.where(causal[None], score, -jnp.inf)
        _, sel = lax.top_k(score, topk)
        kv_sel = c_kv[b_idx, sel]
        att = jnp.einsum('bqhc,bqkc->bhqk', ql, kv_sel).astype(jnp.float32) * scale
        dist = (t[None, :, None] - sel)[:, None]
        att = att - slopes[None, :, None, None] * dist.astype(jnp.float32)
        att = jnp.where(dist >= 0, att, -jnp.inf)
        p = jax.nn.softmax(att, axis=-1).astype(kv_sel.dtype)
        return jnp.einsum('bhqk,bqkc->bqhc', p, kv_sel)

    o_lat = lax.map(block, jnp.arange(S // DSA_QBLOCK))
    o_lat = o_lat.transpose(1, 0, 2, 3, 4).reshape(B, S, N_HEADS, KV_RANK)
    o = jnp.einsum('bshc,hcd->bshd', o_lat, w_uv).reshape(B, S, BRANCH)
    return (o * jax.nn.silu(gate)) @ w_o


def moba_mixer(h, w_in, w_o):
    B, S, _ = h.shape
    proj = h @ w_in
    q, k, v, gate = jnp.split(proj, 4, axis=-1)
    q = q.reshape(B, S, N_HEADS, HEAD_DIM).transpose(0, 2, 1, 3)
    k = k.reshape(B, S, N_HEADS, HEAD_DIM).transpose(0, 2, 1, 3)
    v = v.reshape(B, S, N_HEADS, HEAD_DIM).transpose(0, 2, 1, 3)
    nb = -(-S // MOBA_BLOCK)
    pad = nb * MOBA_BLOCK - S
    kp = jnp.pad(k, ((0, 0), (0, 0), (0, pad), (0, 0)))
    vp = jnp.pad(v, ((0, 0), (0, 0), (0, pad), (0, 0)))
    kb = kp.reshape(B, N_HEADS, nb, MOBA_BLOCK, HEAD_DIM)
    vb = vp.reshape(B, N_HEADS, nb, MOBA_BLOCK, HEAD_DIM)
    k_mean = jnp.mean(kb.astype(jnp.float32), axis=3)
    k_sel = max(1, min(MOBA_TOPK, nb - 1))
    slopes = alibi_slopes()[None, :, None, None]
    scale = HEAD_DIM ** -0.5
    blk_ids = jnp.arange(nb, dtype=jnp.int32)
    bi = jnp.arange(B)[:, None, None, None]
    hi = jnp.arange(N_HEADS)[None, :, None, None]
    in_blk = jnp.arange(MOBA_BLOCK, dtype=jnp.int32)

    def chunk(ci):
        start = ci * MOBA_QCHUNK
        qc = lax.dynamic_slice_in_dim(q, start, MOBA_QCHUNK, axis=2)
        t = start + jnp.arange(MOBA_QCHUNK, dtype=jnp.int32)
        own = start // MOBA_BLOCK
        g = jnp.einsum('bhqd,bhnd->bhqn', qc.astype(jnp.float32), k_mean)
        g = jnp.where(blk_ids < own, g, -jnp.inf)
        _, sel = lax.top_k(g, k_sel)
        ksel = kb[bi, hi, sel]
        vsel = vb[bi, hi, sel]
        s_past = jnp.einsum('bhqd,bhqnkd->bhqnk', qc, ksel).astype(jnp.float32) * scale
        pos_past = sel[..., None] * MOBA_BLOCK + in_blk
        d_past = (t[None, None, :, None, None] - pos_past).astype(jnp.float32)
        s_past = s_past - slopes[..., None] * d_past
        s_past = jnp.where((sel < own)[..., None], s_past, -jnp.inf)
        s_past = s_past.reshape(B, N_HEADS, MOBA_QCHUNK, k_sel * MOBA_BLOCK)
        k_own = lax.dynamic_slice_in_dim(kp, own * MOBA_BLOCK, MOBA_BLOCK, axis=2)
        v_own = lax.dynamic_slice_in_dim(vp, own * MOBA_BLOCK, MOBA_BLOCK, axis=2)
        s_own = jnp.einsum('bhqd,bhkd->bhqk', qc, k_own).astype(jnp.float32) * scale
        d_own = t[:, None] - (own * MOBA_BLOCK + in_blk)[None, :]
        s_own = s_own - slopes * d_own.astype(jnp.float32)[None, None]
        s_own = jnp.where((d_own >= 0)[None, None], s_own, -jnp.inf)
        p = jax.nn.softmax(jnp.concatenate([s_past, s_own], axis=-1), axis=-1).astype(v.dtype)
        p_past = p[..., :k_sel * MOBA_BLOCK]
        p_own = p[..., k_sel * MOBA_BLOCK:]
        vsel = vsel.reshape(B, N_HEADS, MOBA_QCHUNK, k_sel * MOBA_BLOCK, HEAD_DIM)
        return (jnp.einsum('bhqm,bhqmd->bhqd', p_past, vsel)
                + jnp.einsum('bhqk,bhkd->bhqd', p_own, v_own))

    out = lax.map(chunk, jnp.arange(S // MOBA_QCHUNK))
    out = out.transpose(1, 0, 3, 2, 4).reshape(B, S, BRANCH)
    return (out * jax.nn.silu(gate)) @ w_o


def setup_inputs(seed: int = 0) -> dict:
    key = jax.random.key(seed)
    ks = jax.random.split(key, 20)
    f32 = jnp.float32
    nrm = lambda k, shape, s: jax.random.normal(k, shape, f32) * s
    return {
        "x": nrm(ks[0], (BATCH, SEQ, D_MODEL), 1.0),
        "c": nrm(ks[1], (BATCH, D_MODEL), 1.0),
        "ada_w": nrm(ks[2], (DEPTH, D_MODEL, 3 * D_MODEL), D_MODEL ** -0.5),
        "ada_b": nrm(ks[3], (DEPTH, 3 * D_MODEL), 0.02),
        "ln_g": 1.0 + nrm(ks[4], (DEPTH, D_MODEL), 0.02),
        "ln_b": nrm(ks[5], (DEPTH, D_MODEL), 0.02),
        "dsa_w_in": nrm(ks[6], (N_DSA, D_MODEL, DSA_IN), D_MODEL ** -0.5),
        "dsa_g_q": 1.0 + nrm(ks[7], (N_DSA, Q_RANK), 0.02),
        "dsa_g_kv": 1.0 + nrm(ks[8], (N_DSA, KV_RANK), 0.02),
        "dsa_w_uq": nrm(ks[9], (N_DSA, Q_RANK, BRANCH), Q_RANK ** -0.5),
        "dsa_w_qi": nrm(ks[10], (N_DSA, Q_RANK, IDX_HEADS * IDX_DIM), Q_RANK ** -0.5),
        "dsa_w_uk": nrm(ks[11], (N_DSA, N_HEADS, HEAD_DIM, KV_RANK), HEAD_DIM ** -0.5),
        "dsa_w_uv": nrm(ks[12], (N_DSA, N_HEADS, KV_RANK, HEAD_DIM), KV_RANK ** -0.5),
        "dsa_w_o": nrm(ks[13], (N_DSA, BRANCH, D_MODEL), DN_BETA * BRANCH ** -0.5),
        "moba_w_in": nrm(ks[14], (N_MOBA, D_MODEL, MOBA_IN), D_MODEL ** -0.5),
        "moba_w_o": nrm(ks[15], (N_MOBA, BRANCH, D_MODEL), DN_BETA * BRANCH ** -0.5),
    }


def reference(x, c, ada_w, ada_b, ln_g, ln_b, dsa_w_in, dsa_g_q, dsa_g_kv, dsa_w_uq, dsa_w_qi,
              dsa_w_uk, dsa_w_uv, dsa_w_o, moba_w_in, moba_w_o):
    c_act = jax.nn.silu(c)
    for i in range(DEPTH):
        mod = c_act @ ada_w[i] + ada_b[i]
        shift, scl, gate = jnp.split(mod, 3, axis=-1)
        h = x * (1.0 + scl[:, None, :]) + shift[:, None, :]
        j = i // N_MIXERS
        if i % N_MIXERS == 0:
            y = dsa_mixer(h, dsa_w_in[j], dsa_g_q[j], dsa_g_kv[j], dsa_w_uq[j], dsa_w_qi[j],
                          dsa_w_uk[j], dsa_w_uv[j], dsa_w_o[j])
        else:
            y = moba_mixer(h, moba_w_in[j], moba_w_o[j])
        x = layer_norm(DN_ALPHA * x + gate[:, None, :] * y, ln_g[i], ln_b[i])
    return x
```

```python
import functools
import math

import jax
import jax.numpy as jnp
import numpy as np
from jax import lax
from jax.experimental import pallas as pl
from jax.experimental.pallas import tpu as pltpu

N_HEADS = 16
HEAD_DIM = 128
BRANCH = N_HEADS * HEAD_DIM
Q_RANK = 512
KV_RANK = 256
IDX_HEADS = 16
IDX_DIM = 128
TOPK_MAX = 256
MOBA_BLOCK = 256
MOBA_TOPK = 3
LN_EPS = 1e-5

LANES = 128
V7X_VMEM_LIMIT_BYTES = 56 * 1024 * 1024

NEG_BIG = -1e30
INT_MIN = -(2 ** 31)

F32 = jnp.float32
BF16 = jnp.bfloat16


def _alibi_slopes():
    return np.asarray(2.0 ** (-8.0 * np.arange(1, N_HEADS + 1) / N_HEADS), dtype=np.float32)


def _silu(v):
    return v / (1.0 + jnp.exp(-v))


def _nt_dot(a, b):
    return lax.dot_general(a, b, (((1,), (1,)), ((), ())), preferred_element_type=F32)


def _tile_lanes(v, width):
    reps = width // LANES
    if reps == 1:
        return v
    return jnp.concatenate([v] * reps, axis=1)


def _cparams(sem, vmem=None):
    return pltpu.CompilerParams(dimension_semantics=sem, vmem_limit_bytes=vmem)


def _adaln_kernel(c_ref, w_ref, b_ref, o_ref):
    c_act = _silu(c_ref[...])
    prod = w_ref[...] * c_act
    o_ref[...] = jnp.sum(prod, axis=0, keepdims=True) + b_ref[...]


def _adaln_mod(c, ada_w, ada_b, tn=512):
    depth, d, n = ada_w.shape
    c_col = c.reshape(d, 1)
    b3 = ada_b.reshape(depth, 1, n)
    out = pl.pallas_call(
        _adaln_kernel,
        grid=(depth, n // tn),
        in_specs=[
            pl.BlockSpec((d, 1), lambda i, j: (0, 0)),
            pl.BlockSpec((None, d, tn), lambda i, j: (i, 0, j)),
            pl.BlockSpec((None, 1, tn), lambda i, j: (i, 0, j)),
        ],
        out_specs=pl.BlockSpec((None, 1, tn), lambda i, j: (i, 0, j)),
        out_shape=jax.ShapeDtypeStruct((depth, 1, n), F32),
        compiler_params=_cparams(("arbitrary", "arbitrary")),
        name="adaln_mod",
    )(c_col, ada_w, b3)
    return out


def _modulate_kernel(x_ref, scl_ref, shift_ref, o_ref):
    o_ref[...] = (x_ref[...] * (1.0 + scl_ref[...]) + shift_ref[...]).astype(o_ref.dtype)


def _modulate(x, scl, shift, tm=512):
    s, d = x.shape
    return pl.pallas_call(
        _modulate_kernel,
        grid=(s // tm,),
        in_specs=[
            pl.BlockSpec((tm, d), lambda i: (i, 0)),
            pl.BlockSpec((1, d), lambda i: (0, 0)),
            pl.BlockSpec((1, d), lambda i: (0, 0)),
        ],
        out_specs=pl.BlockSpec((tm, d), lambda i: (i, 0)),
        out_shape=jax.ShapeDtypeStruct((s, d), BF16),
        compiler_params=_cparams(("arbitrary",)),
        name="modulate",
    )(x, scl, shift)


def _mm_kernel(x_ref, w_ref, o_ref):
    o_ref[...] = jnp.dot(x_ref[...], w_ref[...], preferred_element_type=F32).astype(o_ref.dtype)


def _matmul(x, w, out_dtype, tm=512, tn=512, name="matmul"):
    m, k = x.shape
    _, n = w.shape
    tm = min(tm, m)
    tn = min(tn, n)
    return pl.pallas_call(
        _mm_kernel,
        grid=(m // tm, n // tn),
        in_specs=[
            pl.BlockSpec((tm, k), lambda i, j: (i, 0)),
            pl.BlockSpec((k, tn), lambda i, j: (0, j)),
        ],
        out_specs=pl.BlockSpec((tm, tn), lambda i, j: (i, j)),
        out_shape=jax.ShapeDtypeStruct((m, n), out_dtype),
        compiler_params=_cparams(("arbitrary", "arbitrary"), V7X_VMEM_LIMIT_BYTES),
        name=name,
    )(x, w)


def _rms(v, g):
    return v * lax.rsqrt(jnp.mean(v * v, axis=-1, keepdims=True) + LN_EPS) * g


def _dsa_latent_kernel(x_ref, wq_ref, wkv_ref, wki_ref, wwi_ref, gq_ref, gkv_ref,
                       cq_ref, ckv_ref, kidx_ref, widx_ref):
    x = x_ref[...]
    cq = jnp.dot(x, wq_ref[...], preferred_element_type=F32)
    cq_ref[...] = _rms(cq, gq_ref[...]).astype(cq_ref.dtype)
    ckv = jnp.dot(x, wkv_ref[...], preferred_element_type=F32)
    ckv_ref[...] = _rms(ckv, gkv_ref[...]).astype(ckv_ref.dtype)
    kidx_ref[...] = jnp.dot(x, wki_ref[...], preferred_element_type=F32).astype(kidx_ref.dtype)
    widx_ref[...] = jnp.dot(x, wwi_ref[...], preferred_element_type=F32)


def _dsa_latent(h, wq, wkv, wki, wwi, gq, gkv, tm=512):
    s, d = h.shape
    full = lambda shape: pl.BlockSpec(shape, lambda i: (0, 0))
    row = lambda n: pl.BlockSpec((tm, n), lambda i: (i, 0))
    return pl.pallas_call(
        _dsa_latent_kernel,
        grid=(s // tm,),
        in_specs=[row(d), full((d, Q_RANK)), full((d, KV_RANK)), full((d, IDX_DIM)),
                  full((d, LANES)), full((1, Q_RANK)), full((1, KV_RANK))],
        out_specs=[row(Q_RANK), row(KV_RANK), row(IDX_DIM), row(LANES)],
        out_shape=[jax.ShapeDtypeStruct((s, Q_RANK), BF16),
                   jax.ShapeDtypeStruct((s, KV_RANK), BF16),
                   jax.ShapeDtypeStruct((s, IDX_DIM), BF16),
                   jax.ShapeDtypeStruct((s, LANES), F32)],
        compiler_params=_cparams(("arbitrary",), V7X_VMEM_LIMIT_BYTES),
        name="dsa_latent",
    )(h, wq, wkv, wki, wwi, gq, gkv)


def _dsa_query_kernel(cq_ref, wqi_ref, wuq_ref, wuk_ref, qidx_ref, qlat_ref):
    cq = cq_ref[...]
    qidx_ref[...] = jnp.dot(cq, wqi_ref[...], preferred_element_type=F32).astype(qidx_ref.dtype)
    q = jnp.dot(cq, wuq_ref[...], preferred_element_type=F32).astype(BF16)
    qlat_ref[...] = jnp.dot(q, wuk_ref[...], preferred_element_type=F32).astype(qlat_ref.dtype)


def _dsa_query(cq, wqi_h, wuq_h, wuk_h, tm=512):
    s, r = cq.shape
    return pl.pallas_call(
        _dsa_query_kernel,
        grid=(s // tm, N_HEADS),
        in_specs=[
            pl.BlockSpec((tm, r), lambda i, h: (i, 0)),
            pl.BlockSpec((None, r, IDX_DIM), lambda i, h: (h, 0, 0)),
            pl.BlockSpec((None, r, HEAD_DIM), lambda i, h: (h, 0, 0)),
            pl.BlockSpec((None, HEAD_DIM, KV_RANK), lambda i, h: (h, 0, 0)),
        ],
        out_specs=[
            pl.BlockSpec((None, tm, IDX_DIM), lambda i, h: (h, i, 0)),
            pl.BlockSpec((None, tm, KV_RANK), lambda i, h: (h, i, 0)),
        ],
        out_shape=[jax.ShapeDtypeStruct((N_HEADS, s, IDX_DIM), BF16),
                   jax.ShapeDtypeStruct((N_HEADS, s, KV_RANK), BF16)],
        compiler_params=_cparams(("arbitrary", "arbitrary")),
        name="dsa_query",
    )(cq, wqi_h, wuq_h, wuk_h)


def _dsa_attn_kernel(qidx_ref, widx_ref, kidx_ref, qlat_ref, ckv_ref, wuv_ref, gate_ref,
                     o_ref, key_scr, wb_scr, p_scr, m_scr, l_scr, a_scr, acc_scr,
                     *, qb, kt, topk, slopes):
    nh = N_HEADS
    i = pl.program_id(0)
    t0 = i * qb
    nkt = (t0 + qb + kt - 1) // kt
    row = t0 + lax.broadcasted_iota(jnp.int32, (qb, kt), 0)
    lane = lax.broadcasted_iota(jnp.int32, (qb, kt), 1)

    w = widx_ref[...]
    for h in range(nh):
        wb_scr[h] = jnp.broadcast_to(w[:, h:h + 1], (qb, LANES))

    qi = qidx_ref[...].reshape(nh * qb, IDX_DIM)

    def idx_body(j, carry):
        c0 = pl.multiple_of(j * kt, kt)
        res = _nt_dot(qi, kidx_ref[pl.ds(c0, kt), :])
        sc = jnp.zeros((qb, kt), F32)
        for h in range(nh):
            sc = sc + _tile_lanes(wb_scr[h], kt) * jnp.maximum(res[h * qb:(h + 1) * qb], 0.0)
        bits = lax.bitcast_convert_type(sc, jnp.int32)
        key = bits ^ ((bits >> 31) & jnp.int32(0x7FFFFFFF))
        key = jnp.where(c0 + lane <= row, key, jnp.int32(INT_MIN))
        key_scr[j] = key
        return carry

    lax.fori_loop(0, nkt, idx_body, 0)

    def bit_body(b, cur):
        cand = cur ^ lax.shift_left(jnp.int32(1), jnp.int32(31) - b)

        def cnt_body(j, cnt):
            tile = key_scr[j]
            for c in range(kt // LANES):
                cnt = cnt + jnp.where(tile[:, c * LANES:(c + 1) * LANES] >= cand, 1.0, 0.0)
            return cnt

        cnt = lax.fori_loop(0, nkt, cnt_body, jnp.zeros((qb, LANES), F32))
        tot = jnp.sum(cnt, axis=1, keepdims=True)
        return jnp.where(tot >= float(topk), cand, cur)

    thr = lax.fori_loop(0, 32, bit_body, jnp.full((qb, LANES), INT_MIN, jnp.int32))
    thr_t = _tile_lanes(thr, kt)

    ql = qlat_ref[...].reshape(nh * qb, KV_RANK)
    m_scr[...] = jnp.full(m_scr.shape, NEG_BIG, F32)
    l_scr[...] = jnp.zeros(l_scr.shape, F32)
    acc_scr[...] = jnp.zeros(acc_scr.shape, F32)

    def att_body(j, carry):
        c0 = pl.multiple_of(j * kt, kt)
        col = c0 + lane
        sel = jnp.logical_and(key_scr[j] >= thr_t, col <= row)
        bias = jnp.where(sel, 0.0, NEG_BIG)
        colf = (c0 - t0 + lax.broadcasted_iota(jnp.int32, (1, kt), 1)).astype(F32)
        ckv = ckv_ref[pl.ds(c0, kt), :]
        logits = _nt_dot(ql, ckv)
        for h in range(nh):
            s = logits[h * qb:(h + 1) * qb] + (slopes[h] * colf) + bias
            m_prev = m_scr[h]
            m_next = jnp.maximum(m_prev, jnp.max(s, axis=1, keepdims=True))
            alpha = jnp.exp(m_prev - m_next)
            p = jnp.exp(s - _tile_lanes(m_next, kt))
            l_scr[h] = alpha * l_scr[h] + jnp.sum(p, axis=1, keepdims=True)
            m_scr[h] = m_next
            a_scr[h] = alpha
            p_scr[h] = p.astype(BF16)
        pv = jnp.dot(p_scr[...].reshape(nh * qb, kt), ckv, preferred_element_type=F32)
        for h in range(nh):
            acc_scr[h] = acc_scr[h] * _tile_lanes(a_scr[h], KV_RANK) + pv[h * qb:(h + 1) * qb]
        return carry

    lax.fori_loop(0, nkt, att_body, 0)

    for h in range(nh):
        o_lat = (acc_scr[h] / _tile_lanes(l_scr[h], KV_RANK)).astype(BF16)
        o = jnp.dot(o_lat, wuv_ref[h], preferred_element_type=F32)
        g = gate_ref[:, h * HEAD_DIM:(h + 1) * HEAD_DIM]
        o_ref[:, h * HEAD_DIM:(h + 1) * HEAD_DIM] = (o * _silu(g)).astype(o_ref.dtype)


def _dsa_attention(qidx, widx, kidx, qlat, ckv, wuv, gate, qb=128, kt=256):
    nh, s, _ = qidx.shape
    topk = min(TOPK_MAX, s // 4)
    nkt_max = s // kt
    slopes = tuple(float(v) for v in _alibi_slopes())
    kern = functools.partial(_dsa_attn_kernel, qb=qb, kt=kt, topk=topk, slopes=slopes)
    return pl.pallas_call(
        kern,
        grid=(s // qb,),
        in_specs=[
            pl.BlockSpec((nh, qb, IDX_DIM), lambda i: (0, i, 0)),
            pl.BlockSpec((qb, LANES), lambda i: (i, 0)),
            pl.BlockSpec((s, IDX_DIM), lambda i: (0, 0)),
            pl.BlockSpec((nh, qb, KV_RANK), lambda i: (0, i, 0)),
            pl.BlockSpec((s, KV_RANK), lambda i: (0, 0)),
            pl.BlockSpec((nh, KV_RANK, HEAD_DIM), lambda i: (0, 0, 0)),
            pl.BlockSpec((qb, BRANCH), lambda i: (i, 0)),
        ],
        out_specs=pl.BlockSpec((qb, BRANCH), lambda i: (i, 0)),
        out_shape=jax.ShapeDtypeStruct((s, BRANCH), BF16),
        scratch_shapes=[
            pltpu.VMEM((nkt_max, qb, kt), jnp.int32),
            pltpu.VMEM((nh, qb, LANES), F32),
            pltpu.VMEM((nh, qb, kt), BF16),
            pltpu.VMEM((nh, qb, LANES), F32),
            pltpu.VMEM((nh, qb, LANES), F32),
            pltpu.VMEM((nh, qb, LANES), F32),
            pltpu.VMEM((nh, qb, KV_RANK), F32),
        ],
        compiler_params=_cparams(("arbitrary",), V7X_VMEM_LIMIT_BYTES),
        name="dsa_attention",
    )(qidx, widx, kidx, qlat, ckv, wuv, gate)


def _moba_attn_kernel(slope_ref, q_ref, k_ref, v_ref, gate_ref, o_ref,
                      kmean_scr, selb_scr, m_scr, l_scr, acc_scr, *, nb):
    blk = MOBA_BLOCK
    i = pl.program_id(1)

    @pl.when(i == 0)
    def _():
        kmean_scr[...] = jnp.zeros(kmean_scr.shape, F32)
        for j in range(nb):
            kj = k_ref[j * blk:(j + 1) * blk, :].astype(F32)
            kmean_scr[j:j + 1, :] = jnp.sum(kj, axis=0, keepdims=True) * (1.0 / blk)

    q = q_ref[...]
    km = kmean_scr[...]
    km_hi = km.astype(BF16)
    km_lo = (km - km_hi.astype(F32)).astype(BF16)
    g = _nt_dot(q, km_hi) + _nt_dot(q, km_lo)
    lane = lax.broadcasted_iota(jnp.int32, (blk, LANES), 1)
    lane_f = lane.astype(F32)
    past = lane < i
    g = jnp.where(past, g, -jnp.inf)
    selb = jnp.full((blk, LANES), NEG_BIG, F32)
    for _ in range(MOBA_TOPK):
        mx = jnp.max(g, axis=1, keepdims=True)
        first = jnp.min(jnp.where(g == mx, lane_f, float(LANES)), axis=1, keepdims=True)
        pick = lane_f == first
        selb = jnp.where(pick, 0.0, selb)
        g = jnp.where(pick, -jnp.inf, g)
    selb_scr[...] = jnp.where(past, selb, NEG_BIG)

    slope = slope_ref[...]
    slope_t = _tile_lanes(slope, blk)
    loc = lax.broadcasted_iota(jnp.int32, (1, blk), 1)
    m_scr[...] = jnp.full(m_scr.shape, NEG_BIG, F32)
    l_scr[...] = jnp.zeros(l_scr.shape, F32)
    acc_scr[...] = jnp.zeros(acc_scr.shape, F32)

    def step(s, vj):
        m_prev = m_scr[...]
        m_next = jnp.maximum(m_prev, jnp.max(s, axis=1, keepdims=True))
        alpha = jnp.exp(m_prev - m_next)
        p = jnp.exp(s - _tile_lanes(m_next, blk))
        l_scr[...] = alpha * l_scr[...] + jnp.sum(p, axis=1, keepdims=True)
        m_scr[...] = m_next
        acc_scr[...] = acc_scr[...] * alpha + jnp.dot(p.astype(BF16), vj,
                                                      preferred_element_type=F32)

    def past_body(j, carry):
        c0 = pl.multiple_of(j * blk, blk)
        kj = k_ref[pl.ds(c0, blk), :]
        vj = v_ref[pl.ds(c0, blk), :]
        rb = jnp.sum(jnp.where(lane == j, selb_scr[...], 0.0), axis=1, keepdims=True)
        colf = ((j - i) * blk + loc).astype(F32)
        s = _nt_dot(q, kj) + slope_t * colf + rb
        step(s, vj)
        return carry

    lax.fori_loop(0, i, past_body, 0)

    c0 = pl.multiple_of(i * blk, blk)
    kj = k_ref[pl.ds(c0, blk), :]
    vj = v_ref[pl.ds(c0, blk), :]
    r2 = lax.broadcasted_iota(jnp.int32, (blk, blk), 0)
    c2 = lax.broadcasted_iota(jnp.int32, (blk, blk), 1)
    s = _nt_dot(q, kj) + slope_t * loc.astype(F32)
    s = jnp.where(c2 <= r2, s, NEG_BIG)
    step(s, vj)

    o = acc_scr[...] / l_scr[...]
    o_ref[...] = (o * _silu(gate_ref[...])).astype(o_ref.dtype)


def _moba_attention(qkv, gate):
    s = qkv.shape[0]
    nb = s // MOBA_BLOCK
    assert nb <= LANES
    blk = MOBA_BLOCK
    slopes = jnp.asarray(np.repeat(_alibi_slopes()[:, None, None], LANES, axis=2))
    kern = functools.partial(_moba_attn_kernel, nb=nb)
    return pl.pallas_call(
        kern,
        grid=(N_HEADS, nb),
        in_specs=[
            pl.BlockSpec((None, 1, LANES), lambda h, i: (h, 0, 0)),
            pl.BlockSpec((blk, HEAD_DIM), lambda h, i: (i, h)),
            pl.BlockSpec((s, HEAD_DIM), lambda h, i: (0, N_HEADS + h)),
            pl.BlockSpec((s, HEAD_DIM), lambda h, i: (0, 2 * N_HEADS + h)),
            pl.BlockSpec((blk, HEAD_DIM), lambda h, i: (i, h)),
        ],
        out_specs=pl.BlockSpec((blk, HEAD_DIM), lambda h, i: (i, h)),
        out_shape=jax.ShapeDtypeStruct((s, BRANCH), BF16),
        scratch_shapes=[
            pltpu.VMEM((LANES, HEAD_DIM), F32),
            pltpu.VMEM((blk, LANES), F32),
            pltpu.VMEM((blk, LANES), F32),
            pltpu.VMEM((blk, LANES), F32),
            pltpu.VMEM((blk, HEAD_DIM), F32),
        ],
        compiler_params=_cparams(("arbitrary", "arbitrary"), V7X_VMEM_LIMIT_BYTES),
        name="moba_attention",
    )(slopes, qkv, qkv, qkv, gate)


def _out_ln_kernel(og_ref, wo_ref, x_ref, gmod_ref, lng_ref, lnb_ref, scl_ref, shift_ref,
                   xo_ref, ho_ref, *, alpha):
    y = jnp.dot(og_ref[...], wo_ref[...], preferred_element_type=F32)
    z = alpha * x_ref[...] + gmod_ref[...] * y
    mu = jnp.mean(z, axis=-1, keepdims=True)
    zc = z - mu
    var = jnp.mean(zc * zc, axis=-1, keepdims=True)
    xn = zc * lax.rsqrt(var + LN_EPS) * lng_ref[...] + lnb_ref[...]
    xo_ref[...] = xn
    ho_ref[...] = (xn * (1.0 + scl_ref[...]) + shift_ref[...]).astype(ho_ref.dtype)


def _out_proj_ln(og, wo, x, gmod, lng, lnb, scl_next, shift_next, alpha, tm=256):
    s, d = x.shape
    vec = pl.BlockSpec((1, d), lambda i: (0, 0))
    row = pl.BlockSpec((tm, d), lambda i: (i, 0))
    kern = functools.partial(_out_ln_kernel, alpha=alpha)
    return pl.pallas_call(
        kern,
        grid=(s // tm,),
        in_specs=[row, pl.BlockSpec((d, d), lambda i: (0, 0)), row, vec, vec, vec, vec, vec],
        out_specs=[row, row],
        out_shape=[jax.ShapeDtypeStruct((s, d), F32), jax.ShapeDtypeStruct((s, d), BF16)],
        compiler_params=_cparams(("arbitrary",), V7X_VMEM_LIMIT_BYTES),
        name="out_proj_ln",
    )(og, wo, x, gmod, lng, lnb, scl_next, shift_next)


def kernel(x, c, ada_w, ada_b, ln_g, ln_b, dsa_w_in, dsa_g_q, dsa_g_kv, dsa_w_uq, dsa_w_qi,
           dsa_w_uk, dsa_w_uv, dsa_w_o, moba_w_in, moba_w_o):
    batch, s, d = x.shape
    assert batch == 1 and d == BRANCH
    depth = ada_w.shape[0]
    alpha = float((2 * depth) ** 0.25)
    scale = HEAD_DIM ** -0.5

    mod = _adaln_mod(c, ada_w, ada_b)
    shift = lambda i: mod[i, :, 0:d]
    scl = lambda i: mod[i, :, d:2 * d]
    gmod = lambda i: mod[i, :, 2 * d:3 * d]

    xs = x.reshape(s, d)
    h = _modulate(xs, scl(0), shift(0))

    o0, o1, o2, o3 = Q_RANK, Q_RANK + KV_RANK, Q_RANK + KV_RANK + IDX_DIM, \
        Q_RANK + KV_RANK + IDX_DIM + IDX_HEADS
    for i in range(depth):
        j = i // 2
        if i % 2 == 0:
            w_in = dsa_w_in[j]
            wq = w_in[:, :o0].astype(BF16)
            wkv = w_in[:, o0:o1].astype(BF16)
            wki = w_in[:, o1:o2].astype(BF16)
            wwi = jnp.pad(w_in[:, o2:o3] * (IDX_HEADS ** -0.5),
                          ((0, 0), (0, LANES - IDX_HEADS))).astype(BF16)
            wg = w_in[:, o3:].astype(BF16)
            wqi_h = (dsa_w_qi[j] * (IDX_DIM ** -0.5)).reshape(Q_RANK, IDX_HEADS, IDX_DIM)
            wqi_h = wqi_h.transpose(1, 0, 2).astype(BF16)
            wuq_h = dsa_w_uq[j].reshape(Q_RANK, N_HEADS, HEAD_DIM).transpose(1, 0, 2).astype(BF16)
            wuk_h = (dsa_w_uk[j] * scale).astype(BF16)
            wuv_h = dsa_w_uv[j].astype(BF16)
            wo = dsa_w_o[j].astype(BF16)

            cq, ckv, kidx, widx = _dsa_latent(h, wq, wkv, wki, wwi,
                                              dsa_g_q[j].reshape(1, -1), dsa_g_kv[j].reshape(1, -1))
            gate = _matmul(h, wg, F32, name="dsa_gate")
            qidx, qlat = _dsa_query(cq, wqi_h, wuq_h, wuk_h)
            og = _dsa_attention(qidx, widx, kidx, qlat, ckv, wuv_h, gate)
        else:
            w_in = moba_w_in[j]
            wqkv = jnp.concatenate([w_in[:, :BRANCH] * scale, w_in[:, BRANCH:3 * BRANCH]],
                                   axis=1).astype(BF16)
            wg = w_in[:, 3 * BRANCH:].astype(BF16)
            wo = moba_w_o[j].astype(BF16)
            qkv = _matmul(h, wqkv, BF16, name="moba_qkv")
            gate = _matmul(h, wg, F32, name="moba_gate")
            og = _moba_attention(qkv, gate)
        nxt = min(i + 1, depth - 1)
        xs, h = _out_proj_ln(og, wo, xs, gmod(i), ln_g[i].reshape(1, d), ln_b[i].reshape(1, d),
                             scl(nxt), shift(nxt), alpha)
    return xs.reshape(batch, s, d)
```

```python
import functools
import math

import jax
import jax.numpy as jnp
import ml_dtypes
import numpy as np
from jax import lax
from jax.experimental import pallas as pl
from jax.experimental.pallas import tpu as pltpu

N_HEADS = 16
HEAD_DIM = 128
BRANCH = N_HEADS * HEAD_DIM
Q_RANK = 512
KV_RANK = 256
IDX_HEADS = 16
IDX_DIM = 128
TOPK_MAX = 256
MOBA_BLOCK = 256
MOBA_TOPK = 3
LN_EPS = 1e-5

LANES = 128
V7X_VMEM_LIMIT_BYTES = 56 * 1024 * 1024

NEG_BIG = -1e30
INT_MIN = -(2 ** 31)

F32 = jnp.float32
BF16 = jnp.bfloat16


def _alibi_slopes():
    return np.asarray(2.0 ** (-8.0 * np.arange(1, N_HEADS + 1) / N_HEADS), dtype=np.float32)


def _silu(v):
    return v / (1.0 + jnp.exp(-v))


def _nt_dot(a, b):
    return lax.dot_general(a, b, (((1,), (1,)), ((), ())), preferred_element_type=F32)


def _tile_lanes(v, width):
    reps = width // LANES
    if reps == 1:
        return v
    return jnp.concatenate([v] * reps, axis=1)


def _cparams(sem, vmem=None):
    return pltpu.CompilerParams(dimension_semantics=sem, vmem_limit_bytes=vmem)


def _adaln_kernel(c_ref, w_ref, b_ref, o_ref):
    c_act = _silu(c_ref[...])
    prod = w_ref[...] * c_act
    o_ref[...] = jnp.sum(prod, axis=0, keepdims=True) + b_ref[...]


def _adaln_mod(c, ada_w, ada_b, tn=512):
    depth, d, n = ada_w.shape
    c_col = c.reshape(d, 1)
    b3 = ada_b.reshape(depth, 1, n)
    out = pl.pallas_call(
        _adaln_kernel,
        grid=(depth, n // tn),
        in_specs=[
            pl.BlockSpec((d, 1), lambda i, j: (0, 0)),
            pl.BlockSpec((None, d, tn), lambda i, j: (i, 0, j)),
            pl.BlockSpec((None, 1, tn), lambda i, j: (i, 0, j)),
        ],
        out_specs=pl.BlockSpec((None, 1, tn), lambda i, j: (i, 0, j)),
        out_shape=jax.ShapeDtypeStruct((depth, 1, n), F32),
        compiler_params=_cparams(("arbitrary", "arbitrary")),
        name="adaln_mod",
    )(c_col, ada_w, b3)
    return out


def _modulate_kernel(x_ref, scl_ref, shift_ref, o_ref):
    o_ref[...] = (x_ref[...] * (1.0 + scl_ref[...]) + shift_ref[...]).astype(o_ref.dtype)


def _modulate(x, scl, shift, tm=512):
    s, d = x.shape
    return pl.pallas_call(
        _modulate_kernel,
        grid=(s // tm,),
        in_specs=[
            pl.BlockSpec((tm, d), lambda i: (i, 0)),
            pl.BlockSpec((1, d), lambda i: (0, 0)),
            pl.BlockSpec((1, d), lambda i: (0, 0)),
        ],
        out_specs=pl.BlockSpec((tm, d), lambda i: (i, 0)),
        out_shape=jax.ShapeDtypeStruct((s, d), BF16),
        compiler_params=_cparams(("arbitrary",)),
        name="modulate",
    )(x, scl, shift)


def _mm_kernel(x_ref, w_ref, o_ref):
    o_ref[...] = jnp.dot(x_ref[...], w_ref[...], preferred_element_type=F32).astype(o_ref.dtype)


def _matmul(x, w, out_dtype, tm=512, tn=512, name="matmul"):
    m, k = x.shape
    _, n = w.shape
    tm = min(tm, m)
    tn = min(tn, n)
    return pl.pallas_call(
        _mm_kernel,
        grid=(m // tm, n // tn),
        in_specs=[
            pl.BlockSpec((tm, k), lambda i, j: (i, 0)),
            pl.BlockSpec((k, tn), lambda i, j: (0, j)),
        ],
        out_specs=pl.BlockSpec((tm, tn), lambda i, j: (i, j)),
        out_shape=jax.ShapeDtypeStruct((m, n), out_dtype),
        compiler_params=_cparams(("arbitrary", "arbitrary"), V7X_VMEM_LIMIT_BYTES),
        name=name,
    )(x, w)


def _rms(v, g):
    return v * lax.rsqrt(jnp.mean(v * v, axis=-1, keepdims=True) + LN_EPS) * g


def _dsa_latent_kernel(x_ref, wq_ref, wkv_ref, wki_ref, wwi_ref, gq_ref, gkv_ref,
                       cq_ref, ckv_ref, kidx_ref, widx_ref):
    x = x_ref[...]
    cq = jnp.dot(x, wq_ref[...], preferred_element_type=F32)
    cq_ref[...] = _rms(cq, gq_ref[...]).astype(cq_ref.dtype)
    ckv = jnp.dot(x, wkv_ref[...], preferred_element_type=F32)
    ckv_ref[...] = _rms(ckv, gkv_ref[...]).astype(ckv_ref.dtype)
    kidx_ref[...] = jnp.dot(x, wki_ref[...], preferred_element_type=F32).astype(kidx_ref.dtype)
    widx_ref[...] = jnp.dot(x, wwi_ref[...], preferred_element_type=F32)


def _dsa_latent(h, wq, wkv, wki, wwi, gq, gkv, tm=512):
    s, d = h.shape
    full = lambda shape: pl.BlockSpec(shape, lambda i: (0, 0))
    row = lambda n: pl.BlockSpec((tm, n), lambda i: (i, 0))
    return pl.pallas_call(
        _dsa_latent_kernel,
        grid=(s // tm,),
        in_specs=[row(d), full((d, Q_RANK)), full((d, KV_RANK)), full((d, IDX_DIM)),
                  full((d, LANES)), full((1, Q_RANK)), full((1, KV_RANK))],
        out_specs=[row(Q_RANK), row(KV_RANK), row(IDX_DIM), row(LANES)],
        out_shape=[jax.ShapeDtypeStruct((s, Q_RANK), BF16),
                   jax.ShapeDtypeStruct((s, KV_RANK), BF16),
                   jax.ShapeDtypeStruct((s, IDX_DIM), BF16),
                   jax.ShapeDtypeStruct((s, LANES), F32)],
        compiler_params=_cparams(("arbitrary",), V7X_VMEM_LIMIT_BYTES),
        name="dsa_latent",
    )(h, wq, wkv, wki, wwi, gq, gkv)


def _dsa_query_kernel(cq_ref, wqi_ref, wuq_ref, wuk_ref, qidx_ref, qlat_ref):
    cq = cq_ref[...]
    qidx_ref[...] = jnp.dot(cq, wqi_ref[...], preferred_element_type=F32).astype(qidx_ref.dtype)
    q = jnp.dot(cq, wuq_ref[...], preferred_element_type=F32).astype(BF16)
    qlat_ref[...] = jnp.dot(q, wuk_ref[...], preferred_element_type=F32).astype(qlat_ref.dtype)


def _dsa_query(cq, wqi_h, wuq_h, wuk_h, tm=512):
    s, r = cq.shape
    return pl.pallas_call(
        _dsa_query_kernel,
        grid=(s // tm, N_HEADS),
        in_specs=[
            pl.BlockSpec((tm, r), lambda i, h: (i, 0)),
            pl.BlockSpec((None, r, IDX_DIM), lambda i, h: (h, 0, 0)),
            pl.BlockSpec((None, r, HEAD_DIM), lambda i, h: (h, 0, 0)),
            pl.BlockSpec((None, HEAD_DIM, KV_RANK), lambda i, h: (h, 0, 0)),
        ],
        out_specs=[
            pl.BlockSpec((None, tm, IDX_DIM), lambda i, h: (h, i, 0)),
            pl.BlockSpec((None, tm, KV_RANK), lambda i, h: (h, i, 0)),
        ],
        out_shape=[jax.ShapeDtypeStruct((N_HEADS, s, IDX_DIM), BF16),
                   jax.ShapeDtypeStruct((N_HEADS, s, KV_RANK), BF16)],
        compiler_params=_cparams(("arbitrary", "arbitrary")),
        name="dsa_query",
    )(cq, wqi_h, wuq_h, wuk_h)


def _dsa_attn_kernel(qidx_ref, widx_ref, kidx_ref, qlat_ref, ckv_ref, wuv_ref, gate_ref,
                     o_ref, key_scr, wb_scr, p_scr, m_scr, l_scr, a_scr, acc_scr,
                     *, qb, kt, topk, slopes):
    nh = N_HEADS
    i = pl.program_id(0)
    t0 = i * qb
    nkt = (t0 + qb + kt - 1) // kt
    row = t0 + lax.broadcasted_iota(jnp.int32, (qb, kt), 0)
    lane = lax.broadcasted_iota(jnp.int32, (qb, kt), 1)

    w = widx_ref[...]
    for h in range(nh):
        wb_scr[h] = jnp.broadcast_to(w[:, h:h + 1], (qb, LANES))

    qi = qidx_ref[...].reshape(nh * qb, IDX_DIM)

    def idx_body(j, carry):
        c0 = pl.multiple_of(j * kt, kt)
        res = _nt_dot(qi, kidx_ref[pl.ds(c0, kt), :])
        sc = jnp.zeros((qb, kt), F32)
        for h in range(nh):
            sc = sc + _tile_lanes(wb_scr[h], kt) * jnp.maximum(res[h * qb:(h + 1) * qb], 0.0)
        bits = lax.bitcast_convert_type(sc, jnp.int32)
        key = bits ^ ((bits >> 31) & jnp.int32(0x7FFFFFFF))
        key = jnp.where(c0 + lane <= row, key, jnp.int32(INT_MIN))
        key_scr[j] = key
        return carry

    lax.fori_loop(0, nkt, idx_body, 0)

    def bit_body(b, cur):
        cand = cur ^ lax.shift_left(jnp.int32(1), jnp.int32(31) - b)

        def cnt_body(j, cnt):
            tile = key_scr[j]
            for c in range(kt // LANES):
                cnt = cnt + jnp.where(tile[:, c * LANES:(c + 1) * LANES] >= cand, 1.0, 0.0)
            return cnt

        cnt = lax.fori_loop(0, nkt, cnt_body, jnp.zeros((qb, LANES), F32))
        tot = jnp.sum(cnt, axis=1, keepdims=True)
        return jnp.where(tot >= float(topk), cand, cur)

    thr = lax.fori_loop(0, 32, bit_body, jnp.full((qb, LANES), INT_MIN, jnp.int32))
    thr_t = _tile_lanes(thr, kt)

    ql = qlat_ref[...].reshape(nh * qb, KV_RANK)
    m_scr[...] = jnp.full(m_scr.shape, NEG_BIG, F32)
    l_scr[...] = jnp.zeros(l_scr.shape, F32)
    acc_scr[...] = jnp.zeros(acc_scr.shape, F32)

    def att_body(j, carry):
        c0 = pl.multiple_of(j * kt, kt)
        col = c0 + lane
        sel = jnp.logical_and(key_scr[j] >= thr_t, col <= row)
        bias = jnp.where(sel, 0.0, NEG_BIG)
        colf = (c0 - t0 + lax.broadcasted_iota(jnp.int32, (1, kt), 1)).astype(F32)
        ckv = ckv_ref[pl.ds(c0, kt), :]
        logits = _nt_dot(ql, ckv)
        for h in range(nh):
            s = logits[h * qb:(h + 1) * qb] + (slopes[h] * colf) + bias
            m_prev = m_scr[h]
            m_next = jnp.maximum(m_prev, jnp.max(s, axis=1, keepdims=True))
            alpha = jnp.exp(m_prev - m_next)
            p = jnp.exp(s - _tile_lanes(m_next, kt))
            l_scr[h] = alpha * l_scr[h] + jnp.sum(p, axis=1, keepdims=True)
            m_scr[h] = m_next
            a_scr[h] = alpha
            p_scr[h] = p.astype(BF16)
        pv = jnp.dot(p_scr[...].reshape(nh * qb, kt), ckv, preferred_element_type=F32)
        for h in range(nh):
            acc_scr[h] = acc_scr[h] * _tile_lanes(a_scr[h], KV_RANK) + pv[h * qb:(h + 1) * qb]
        return carry

    lax.fori_loop(0, nkt, att_body, 0)

    for h in range(nh):
        o_lat = (acc_scr[h] / _tile_lanes(l_scr[h], KV_RANK)).astype(BF16)
        o = jnp.dot(o_lat, wuv_ref[h], preferred_element_type=F32)
        g = gate_ref[:, h * HEAD_DIM:(h + 1) * HEAD_DIM]
        o_ref[:, h * HEAD_DIM:(h + 1) * HEAD_DIM] = (o * _silu(g)).astype(o_ref.dtype)


def _dsa_attention(qidx, widx, kidx, qlat, ckv, wuv, gate, qb=128, kt=256):
    nh, s, _ = qidx.shape
    topk = min(TOPK_MAX, s // 4)
    nkt_max = s // kt
    slopes = tuple(float(v) for v in _alibi_slopes())
    kern = functools.partial(_dsa_attn_kernel, qb=qb, kt=kt, topk=topk, slopes=slopes)
    return pl.pallas_call(
        kern,
        grid=(s // qb,),
        in_specs=[
            pl.BlockSpec((nh, qb, IDX_DIM), lambda i: (0, i, 0)),
            pl.BlockSpec((qb, LANES), lambda i: (i, 0)),
            pl.BlockSpec((s, IDX_DIM), lambda i: (0, 0)),
            pl.BlockSpec((nh, qb, KV_RANK), lambda i: (0, i, 0)),
            pl.BlockSpec((s, KV_RANK), lambda i: (0, 0)),
            pl.BlockSpec((nh, KV_RANK, HEAD_DIM), lambda i: (0, 0, 0)),
            pl.BlockSpec((qb, BRANCH), lambda i: (i, 0)),
        ],
        out_specs=pl.BlockSpec((qb, BRANCH), lambda i: (i, 0)),
        out_shape=jax.ShapeDtypeStruct((s, BRANCH), BF16),
        scratch_shapes=[
            pltpu.VMEM((nkt_max, qb, kt), jnp.int32),
            pltpu.VMEM((nh, qb, LANES), F32),
            pltpu.VMEM((nh, qb, kt), BF16),
            pltpu.VMEM((nh, qb, LANES), F32),
            pltpu.VMEM((nh, qb, LANES), F32),
            pltpu.VMEM((nh, qb, LANES), F32),
            pltpu.VMEM((nh, qb, KV_RANK), F32),
        ],
        compiler_params=_cparams(("arbitrary",), V7X_VMEM_LIMIT_BYTES),
        name="dsa_attention",
    )(qidx, widx, kidx, qlat, ckv, wuv, gate)


MOBA_POS_SPLIT = 64
MOBA_SEL_LANE0 = 8
MOBA_MASK = -(2.0 ** 100)


def _moba_key_consts(s):
    pos = np.arange(s)
    kc = np.zeros((s, LANES), np.float32)
    kc[:, 0:3] = (MOBA_POS_SPLIT * (pos // MOBA_POS_SPLIT))[:, None]
    kc[:, 3:6] = (pos % MOBA_POS_SPLIT)[:, None]
    kc[pos, MOBA_SEL_LANE0 + pos // MOBA_BLOCK] = 1.0
    return jnp.asarray(kc, BF16)


def _moba_query_consts():
    bf = lambda a: a.astype(ml_dtypes.bfloat16).astype(np.float32)
    v = (_alibi_slopes() * np.float32(math.log2(math.e))).astype(np.float32)
    hi = bf(v)
    mid = bf(v - hi)
    lo = bf(v - hi - mid)
    qc = np.zeros((N_HEADS, 1, LANES), np.float32)
    for p, piece in enumerate((hi, mid, lo)):
        qc[:, 0, p] = piece
        qc[:, 0, 3 + p] = piece
    return jnp.asarray(qc)


def _moba_attn_kernel(qc_ref, q_ref, k_ref, v_ref, kc_ref, gate_ref, o_ref,
                      kaug_scr, kmean_scr, m_scr, l_scr, acc_scr, *, nb, qt, kt):
    blk = MOBA_BLOCK
    i = pl.program_id(1)
    sub = qt // blk
    tiles = qt // kt

    @pl.when(i == 0)
    def _():
        kaug_scr[:, 0:HEAD_DIM] = k_ref[...]
        kaug_scr[:, HEAD_DIM:] = kc_ref[...]
        kmean_scr[...] = jnp.zeros(kmean_scr.shape, F32)
        for j in range(nb):
            kj = k_ref[j * blk:(j + 1) * blk, :].astype(F32)
            r = MOBA_SEL_LANE0 + j
            kmean_scr[r:r + 1, :] = jnp.sum(kj, axis=0, keepdims=True) * (1.0 / blk)

    q = q_ref[...]
    km = kmean_scr[...]
    km_hi = km.astype(BF16)
    km_lo = (km - km_hi.astype(F32)).astype(BF16)
    g = _nt_dot(q, km_hi) + _nt_dot(q, km_lo)
    lane = lax.broadcasted_iota(jnp.int32, (qt, LANES), 1)
    lane_f = lane.astype(F32)
    blk_id = lane - MOBA_SEL_LANE0
    own = i * sub + lax.broadcasted_iota(jnp.int32, (qt, LANES), 0) // blk
    past = jnp.logical_and(blk_id >= 0, blk_id < own)
    g = jnp.where(past, g, -jnp.inf)
    selb = jnp.full((qt, LANES), MOBA_MASK, F32)
    for _ in range(MOBA_TOPK):
        mx = jnp.max(g, axis=1, keepdims=True)
        first = jnp.min(jnp.where(g == mx, lane_f, float(LANES)), axis=1, keepdims=True)
        pick = lane_f == first
        selb = jnp.where(pick, 0.0, selb)
        g = jnp.where(pick, -jnp.inf, g)
    selb = jnp.where(past, selb, MOBA_MASK)
    selb = jnp.where(blk_id == own, 0.0, selb)
    aug = jnp.where(lane < MOBA_SEL_LANE0, qc_ref[...], selb)
    q_aug = jnp.concatenate([q, aug.astype(BF16)], axis=1)

    m_scr[...] = jnp.full(m_scr.shape, MOBA_MASK, F32)
    l_scr[...] = jnp.zeros(l_scr.shape, F32)
    acc_scr[...] = jnp.zeros(acc_scr.shape, F32)

    def tile(c0, causal):
        s = _nt_dot(q_aug, kaug_scr[pl.ds(c0, kt), :])
        if causal:
            qpos = i * qt + lax.broadcasted_iota(jnp.int32, (qt, kt), 0)
            kpos = c0 + lax.broadcasted_iota(jnp.int32, (qt, kt), 1)
            s = jnp.where(kpos <= qpos, s, MOBA_MASK)
        m_prev = m_scr[...]
        m_next = jnp.maximum(m_prev, jnp.max(s, axis=1, keepdims=True))
        alpha = jnp.exp2(m_prev - m_next)
        p = jnp.exp2(s - _tile_lanes(m_next, kt))
        l_scr[...] = alpha * l_scr[...] + jnp.sum(p, axis=1, keepdims=True)
        m_scr[...] = m_next
        acc_scr[...] = acc_scr[...] * alpha + jnp.dot(
            p.astype(BF16), v_ref[pl.ds(c0, kt), :], preferred_element_type=F32)

    def past_body(j, carry):
        tile(pl.multiple_of(j * kt, kt), False)
        return carry

    lax.fori_loop(0, i * tiles, past_body, 0)
    for d in range(tiles):
        tile(pl.multiple_of((i * tiles + d) * kt, kt), True)

    o = acc_scr[...] / l_scr[...]
    o_ref[...] = (o * _silu(gate_ref[...])).astype(o_ref.dtype)


def _moba_attention(qkv, gate, qt=1024, kt=512):
    s = qkv.shape[0]
    nb = s // MOBA_BLOCK
    qt = min(qt, s)
    kt = min(kt, qt)
    assert MOBA_SEL_LANE0 + nb <= LANES and s < MOBA_POS_SPLIT * 256
    assert s % qt == 0 and qt % kt == 0 and kt % MOBA_BLOCK == 0
    kern = functools.partial(_moba_attn_kernel, nb=nb, qt=qt, kt=kt)
    return pl.pallas_call(
        kern,
        grid=(N_HEADS, s // qt),
        in_specs=[
            pl.BlockSpec((None, 1, LANES), lambda h, i: (h, 0, 0)),
            pl.BlockSpec((qt, HEAD_DIM), lambda h, i: (i, h)),
            pl.BlockSpec((s, HEAD_DIM), lambda h, i: (0, N_HEADS + h)),
            pl.BlockSpec((s, HEAD_DIM), lambda h, i: (0, 2 * N_HEADS + h)),
            pl.BlockSpec((s, LANES), lambda h, i: (0, 0)),
            pl.BlockSpec((qt, HEAD_DIM), lambda h, i: (i, h)),
        ],
        out_specs=pl.BlockSpec((qt, HEAD_DIM), lambda h, i: (i, h)),
        out_shape=jax.ShapeDtypeStruct((s, BRANCH), BF16),
        scratch_shapes=[
            pltpu.VMEM((s, 2 * HEAD_DIM), BF16),
            pltpu.VMEM((LANES, HEAD_DIM), F32),
            pltpu.VMEM((qt, LANES), F32),
            pltpu.VMEM((qt, LANES), F32),
            pltpu.VMEM((qt, HEAD_DIM), F32),
        ],
        compiler_params=_cparams(("arbitrary", "arbitrary"), V7X_VMEM_LIMIT_BYTES),
        name="moba_attention",
    )(_moba_query_consts(), qkv, qkv, qkv, _moba_key_consts(s), gate)


def _out_ln_kernel(og_ref, wo_ref, x_ref, gmod_ref, lng_ref, lnb_ref, scl_ref, shift_ref,
                   xo_ref, ho_ref, *, alpha):
    y = jnp.dot(og_ref[...], wo_ref[...], preferred_element_type=F32)
    z = alpha * x_ref[...] + gmod_ref[...] * y
    mu = jnp.mean(z, axis=-1, keepdims=True)
    zc = z - mu
    var = jnp.mean(zc * zc, axis=-1, keepdims=True)
    xn = zc * lax.rsqrt(var + LN_EPS) * lng_ref[...] + lnb_ref[...]
    xo_ref[...] = xn
    ho_ref[...] = (xn * (1.0 + scl_ref[...]) + shift_ref[...]).astype(ho_ref.dtype)


def _out_proj_ln(og, wo, x, gmod, lng, lnb, scl_next, shift_next, alpha, tm=256):
    s, d = x.shape
    vec = pl.BlockSpec((1, d), lambda i: (0, 0))
    row = pl.BlockSpec((tm, d), lambda i: (i, 0))
    kern = functools.partial(_out_ln_kernel, alpha=alpha)
    return pl.pallas_call(
        kern,
        grid=(s // tm,),
        in_specs=[row, pl.BlockSpec((d, d), lambda i: (0, 0)), row, vec, vec, vec, vec, vec],
        out_specs=[row, row],
        out_shape=[jax.ShapeDtypeStruct((s, d), F32), jax.ShapeDtypeStruct((s, d), BF16)],
        compiler_params=_cparams(("arbitrary",), V7X_VMEM_LIMIT_BYTES),
        name="out_proj_ln",
    )(og, wo, x, gmod, lng, lnb, scl_next, shift_next)


def kernel(x, c, ada_w, ada_b, ln_g, ln_b, dsa_w_in, dsa_g_q, dsa_g_kv, dsa_w_uq, dsa_w_qi,
           dsa_w_uk, dsa_w_uv, dsa_w_o, moba_w_in, moba_w_o):
    batch, s, d = x.shape
    assert batch == 1 and d == BRANCH
    depth = ada_w.shape[0]
    alpha = float((2 * depth) ** 0.25)
    scale = HEAD_DIM ** -0.5

    mod = _adaln_mod(c, ada_w, ada_b)
    shift = lambda i: mod[i, :, 0:d]
    scl = lambda i: mod[i, :, d:2 * d]
    gmod = lambda i: mod[i, :, 2 * d:3 * d]

    xs = x.reshape(s, d)
    h = _modulate(xs, scl(0), shift(0))

    o0, o1, o2, o3 = Q_RANK, Q_RANK + KV_RANK, Q_RANK + KV_RANK + IDX_DIM, \
        Q_RANK + KV_RANK + IDX_DIM + IDX_HEADS
    for i in range(depth):
        j = i // 2
        if i % 2 == 0:
            w_in = dsa_w_in[j]
            wq = w_in[:, :o0].astype(BF16)
            wkv = w_in[:, o0:o1].astype(BF16)
            wki = w_in[:, o1:o2].astype(BF16)
            wwi = jnp.pad(w_in[:, o2:o3] * (IDX_HEADS ** -0.5),
                          ((0, 0), (0, LANES - IDX_HEADS))).astype(BF16)
            wg = w_in[:, o3:].astype(BF16)
            wqi_h = (dsa_w_qi[j] * (IDX_DIM ** -0.5)).reshape(Q_RANK, IDX_HEADS, IDX_DIM)
            wqi_h = wqi_h.transpose(1, 0, 2).astype(BF16)
            wuq_h = dsa_w_uq[j].reshape(Q_RANK, N_HEADS, HEAD_DIM).transpose(1, 0, 2).astype(BF16)
            wuk_h = (dsa_w_uk[j] * scale).astype(BF16)
            wuv_h = dsa_w_uv[j].astype(BF16)
            wo = dsa_w_o[j].astype(BF16)

            cq, ckv, kidx, widx = _dsa_latent(h, wq, wkv, wki, wwi,
                                              dsa_g_q[j].reshape(1, -1), dsa_g_kv[j].reshape(1, -1))
            gate = _matmul(h, wg, F32, name="dsa_gate")
            qidx, qlat = _dsa_query(cq, wqi_h, wuq_h, wuk_h)
            og = _dsa_attention(qidx, widx, kidx, qlat, ckv, wuv_h, gate)
        else:
            w_in = moba_w_in[j]
            wqkv = jnp.concatenate([w_in[:, :BRANCH] * (scale * math.log2(math.e)),
                                    w_in[:, BRANCH:3 * BRANCH]], axis=1).astype(BF16)
            wg = w_in[:, 3 * BRANCH:].astype(BF16)
            wo = moba_w_o[j].astype(BF16)
            qkv = _matmul(h, wqkv, BF16, name="moba_qkv")
            gate = _matmul(h, wg, F32, name="moba_gate")
            og = _moba_attention(qkv, gate)
        nxt = min(i + 1, depth - 1)
        xs, h = _out_proj_ln(og, wo, xs, gmod(i), ln_g[i].reshape(1, d), ln_b[i].reshape(1, d),
                             scl(nxt), shift(nxt), alpha)
    return xs.reshape(batch, s, d)
```

```python
import functools
import math

import jax
import jax.numpy as jnp
import ml_dtypes
import numpy as np
from jax import lax
from jax.experimental import pallas as pl
from jax.experimental.pallas import tpu as pltpu

N_HEADS = 16
HEAD_DIM = 128
BRANCH = N_HEADS * HEAD_DIM
Q_RANK = 512
KV_RANK = 256
IDX_HEADS = 16
IDX_DIM = 128
TOPK_MAX = 256
MOBA_BLOCK = 256
MOBA_TOPK = 3
LN_EPS = 1e-5

LANES = 128
V7X_VMEM_LIMIT_BYTES = 56 * 1024 * 1024

NEG_BIG = -1e30
INT_MIN = -(2 ** 31)

F32 = jnp.float32
BF16 = jnp.bfloat16


def _alibi_slopes():
    return np.asarray(2.0 ** (-8.0 * np.arange(1, N_HEADS + 1) / N_HEADS), dtype=np.float32)


def _silu(v):
    return v / (1.0 + jnp.exp(-v))


def _nt_dot(a, b):
    return lax.dot_general(a, b, (((1,), (1,)), ((), ())), preferred_element_type=F32)


def _tile_lanes(v, width):
    reps = width // LANES
    if reps == 1:
        return v
    return jnp.concatenate([v] * reps, axis=1)


def _cparams(sem, vmem=None):
    return pltpu.CompilerParams(dimension_semantics=sem, vmem_limit_bytes=vmem)


def _adaln_kernel(c_ref, w_ref, b_ref, o_ref):
    c_act = _silu(c_ref[...])
    prod = w_ref[...] * c_act
    o_ref[...] = jnp.sum(prod, axis=0, keepdims=True) + b_ref[...]


def _adaln_mod(c, ada_w, ada_b, tn=512):
    depth, d, n = ada_w.shape
    c_col = c.reshape(d, 1)
    b3 = ada_b.reshape(depth, 1, n)
    out = pl.pallas_call(
        _adaln_kernel,
        grid=(depth, n // tn),
        in_specs=[
            pl.BlockSpec((d, 1), lambda i, j: (0, 0)),
            pl.BlockSpec((None, d, tn), lambda i, j: (i, 0, j)),
            pl.BlockSpec((None, 1, tn), lambda i, j: (i, 0, j)),
        ],
        out_specs=pl.BlockSpec((None, 1, tn), lambda i, j: (i, 0, j)),
        out_shape=jax.ShapeDtypeStruct((depth, 1, n), F32),
        compiler_params=_cparams(("arbitrary", "arbitrary")),
        name="adaln_mod",
    )(c_col, ada_w, b3)
    return out


def _modulate_kernel(x_ref, scl_ref, shift_ref, o_ref):
    o_ref[...] = (x_ref[...] * (1.0 + scl_ref[...]) + shift_ref[...]).astype(o_ref.dtype)


def _modulate(x, scl, shift, tm=512):
    s, d = x.shape
    return pl.pallas_call(
        _modulate_kernel,
        grid=(s // tm,),
        in_specs=[
            pl.BlockSpec((tm, d), lambda i: (i, 0)),
            pl.BlockSpec((1, d), lambda i: (0, 0)),
            pl.BlockSpec((1, d), lambda i: (0, 0)),
        ],
        out_specs=pl.BlockSpec((tm, d), lambda i: (i, 0)),
        out_shape=jax.ShapeDtypeStruct((s, d), BF16),
        compiler_params=_cparams(("arbitrary",)),
        name="modulate",
    )(x, scl, shift)


def _mm_kernel(x_ref, w_ref, o_ref):
    o_ref[...] = jnp.dot(x_ref[...], w_ref[...], preferred_element_type=F32).astype(o_ref.dtype)


def _matmul(x, w, out_dtype, tm=512, tn=512, name="matmul"):
    m, k = x.shape
    _, n = w.shape
    tm = min(tm, m)
    tn = min(tn, n)
    return pl.pallas_call(
        _mm_kernel,
        grid=(m // tm, n // tn),
        in_specs=[
            pl.BlockSpec((tm, k), lambda i, j: (i, 0)),
            pl.BlockSpec((k, tn), lambda i, j: (0, j)),
        ],
        out_specs=pl.BlockSpec((tm, tn), lambda i, j: (i, j)),
        out_shape=jax.ShapeDtypeStruct((m, n), out_dtype),
        compiler_params=_cparams(("arbitrary", "arbitrary"), V7X_VMEM_LIMIT_BYTES),
        name=name,
    )(x, w)


def _rms(v, g):
    return v * lax.rsqrt(jnp.mean(v * v, axis=-1, keepdims=True) + LN_EPS) * g


def _dsa_latent_kernel(x_ref, wq_ref, wkv_ref, wki_ref, wwi_ref, gq_ref, gkv_ref,
                       cq_ref, ckv_ref, kidx_ref, widx_ref):
    x = x_ref[...]
    cq = jnp.dot(x, wq_ref[...], preferred_element_type=F32)
    cq_ref[...] = _rms(cq, gq_ref[...]).astype(cq_ref.dtype)
    ckv = jnp.dot(x, wkv_ref[...], preferred_element_type=F32)
    ckv_ref[...] = _rms(ckv, gkv_ref[...]).astype(ckv_ref.dtype)
    kidx_ref[...] = jnp.dot(x, wki_ref[...], preferred_element_type=F32).astype(kidx_ref.dtype)
    widx_ref[...] = jnp.dot(x, wwi_ref[...], preferred_element_type=F32)


def _dsa_latent(h, wq, wkv, wki, wwi, gq, gkv, tm=512):
    s, d = h.shape
    full = lambda shape: pl.BlockSpec(shape, lambda i: (0, 0))
    row = lambda n: pl.BlockSpec((tm, n), lambda i: (i, 0))
    return pl.pallas_call(
        _dsa_latent_kernel,
        grid=(s // tm,),
        in_specs=[row(d), full((d, Q_RANK)), full((d, KV_RANK)), full((d, IDX_DIM)),
                  full((d, LANES)), full((1, Q_RANK)), full((1, KV_RANK))],
        out_specs=[row(Q_RANK), row(KV_RANK), row(IDX_DIM), row(LANES)],
        out_shape=[jax.ShapeDtypeStruct((s, Q_RANK), BF16),
                   jax.ShapeDtypeStruct((s, KV_RANK), BF16),
                   jax.ShapeDtypeStruct((s, IDX_DIM), BF16),
                   jax.ShapeDtypeStruct((s, LANES), F32)],
        compiler_params=_cparams(("arbitrary",), V7X_VMEM_LIMIT_BYTES),
        name="dsa_latent",
    )(h, wq, wkv, wki, wwi, gq, gkv)


def _dsa_query_kernel(cq_ref, wqi_ref, wuq_ref, wuk_ref, qidx_ref, qlat_ref):
    cq = cq_ref[...]
    qidx_ref[...] = jnp.dot(cq, wqi_ref[...], preferred_element_type=F32).astype(qidx_ref.dtype)
    q = jnp.dot(cq, wuq_ref[...], preferred_element_type=F32).astype(BF16)
    qlat_ref[...] = jnp.dot(q, wuk_ref[...], preferred_element_type=F32).astype(qlat_ref.dtype)


def _dsa_query(cq, wqi_h, wuq_h, wuk_h, tm=512):
    s, r = cq.shape
    return pl.pallas_call(
        _dsa_query_kernel,
        grid=(s // tm, N_HEADS),
        in_specs=[
            pl.BlockSpec((tm, r), lambda i, h: (i, 0)),
            pl.BlockSpec((None, r, IDX_DIM), lambda i, h: (h, 0, 0)),
            pl.BlockSpec((None, r, HEAD_DIM), lambda i, h: (h, 0, 0)),
            pl.BlockSpec((None, HEAD_DIM, KV_RANK), lambda i, h: (h, 0, 0)),
        ],
        out_specs=[
            pl.BlockSpec((None, tm, IDX_DIM), lambda i, h: (h, i, 0)),
            pl.BlockSpec((None, tm, KV_RANK), lambda i, h: (h, i, 0)),
        ],
        out_shape=[jax.ShapeDtypeStruct((N_HEADS, s, IDX_DIM), BF16),
                   jax.ShapeDtypeStruct((N_HEADS, s, KV_RANK), BF16)],
        compiler_params=_cparams(("arbitrary", "arbitrary")),
        name="dsa_query",
    )(cq, wqi_h, wuq_h, wuk_h)


def _dsa_attn_kernel(qidx_ref, widx_ref, kidx_ref, qlat_ref, ckv_ref, wuv_ref, gate_ref,
                     o_ref, key_scr, wb_scr, lga_scr, lgb_scr, mta_scr, mtb_scr, p_scr, m_scr, l_scr,
                     a_scr, acc_scr,
                     *, qb, kt, nkt_max, topk, slopes):
    nh = N_HEADS
    i = pl.program_id(0)
    t0 = i * qb
    nkt = (t0 + qb + kt - 1) // kt
    row = t0 + lax.broadcasted_iota(jnp.int32, (qb, kt), 0)
    lane = lax.broadcasted_iota(jnp.int32, (qb, kt), 1)

    w = widx_ref[...]
    for h in range(nh):
        wb_scr[h] = jnp.broadcast_to(w[:, h:h + 1], (qb, LANES))

    qi = qidx_ref[...].reshape(nh * qb, IDX_DIM)

    def idx_body(j, carry):
        c0 = pl.multiple_of(j * kt, kt)
        res = _nt_dot(qi, kidx_ref[pl.ds(c0, kt), :])
        sc = jnp.zeros((qb, kt), F32)
        for h in range(nh):
            sc = sc + _tile_lanes(wb_scr[h], kt) * jnp.maximum(res[h * qb:(h + 1) * qb], 0.0)
        bits = lax.bitcast_convert_type(sc, jnp.int32)
        key = bits ^ ((bits >> 31) & jnp.int32(0x7FFFFFFF))
        key = jnp.where(c0 + lane <= row, key, jnp.int32(INT_MIN))
        key_scr[j] = key
        return carry

    lax.fori_loop(0, nkt, idx_body, 0)

    def bit_body(b, cur):
        cand = cur ^ lax.shift_left(jnp.int32(1), jnp.int32(31) - b)

        def cnt_body(j, cnt):
            tile = key_scr[j]
            for c in range(kt // LANES):
                cnt = cnt + jnp.where(tile[:, c * LANES:(c + 1) * LANES] >= cand, 1.0, 0.0)
            return cnt

        cnt = lax.fori_loop(0, nkt, cnt_body, jnp.zeros((qb, LANES), F32))
        tot = jnp.sum(cnt, axis=1, keepdims=True)
        return jnp.where(tot >= float(topk), cand, cur)

    thr = lax.fori_loop(0, 32, bit_body, jnp.full((qb, LANES), INT_MIN, jnp.int32))
    thr_t = _tile_lanes(thr, kt)

    ql = qlat_ref[...].reshape(nh * qb, KV_RANK)
    m_scr[...] = jnp.full(m_scr.shape, NEG_BIG, F32)
    l_scr[...] = jnp.zeros(l_scr.shape, F32)
    acc_scr[...] = jnp.zeros(acc_scr.shape, F32)

    def tile_start(j):
        return pl.multiple_of(jnp.minimum(j, nkt_max - 1) * kt, kt)

    def logits_into(j, dst, mt_dst):
        col = j * kt + lane
        sel = jnp.logical_and(key_scr[jnp.minimum(j, nkt_max - 1)] >= thr_t, col <= row)
        bias = jnp.where(sel, 0.0, NEG_BIG)
        colf = (j * kt - t0 + lax.broadcasted_iota(jnp.int32, (1, kt), 1)).astype(F32)
        lg = _nt_dot(ql, ckv_ref[pl.ds(tile_start(j), kt), :])
        for h in range(nh):
            s = lg[h * qb:(h + 1) * qb, :] + slopes[h] * colf + bias
            dst[h * qb:(h + 1) * qb, :] = s
            mt_dst[h] = jnp.broadcast_to(jnp.max(s, axis=1, keepdims=True), (qb, LANES))

    def softmax_pv(j, src, mt_src):
        for h in range(nh):
            s = src[h * qb:(h + 1) * qb, :]
            m_prev = m_scr[h]
            m_next = jnp.maximum(m_prev, mt_src[h])
            alpha = jnp.exp2(m_prev - m_next)
            p = jnp.exp2(s - _tile_lanes(m_next, kt))
            l_scr[h] = alpha * l_scr[h] + jnp.sum(p, axis=1, keepdims=True)
            m_scr[h] = m_next
            a_scr[h] = alpha
            p_scr[h] = p.astype(BF16)
        pv = jnp.dot(p_scr[...].reshape(nh * qb, kt), ckv_ref[pl.ds(tile_start(j), kt), :],
                     preferred_element_type=F32)
        a = _tile_lanes(a_scr[...].reshape(nh * qb, LANES), KV_RANK)
        acc_scr[...] = (acc_scr[...].reshape(nh * qb, KV_RANK) * a + pv).reshape(nh, qb, KV_RANK)

    def pair_body(pp, carry):
        j = 2 * pp
        logits_into(j + 1, lgb_scr, mtb_scr)
        softmax_pv(j, lga_scr, mta_scr)
        logits_into(j + 2, lga_scr, mta_scr)
        softmax_pv(j + 1, lgb_scr, mtb_scr)
        return carry

    logits_into(0, lga_scr, mta_scr)
    lax.fori_loop(0, (nkt + 1) // 2, pair_body, 0)

    for h in range(nh):
        o_lat = (acc_scr[h] / _tile_lanes(l_scr[h], KV_RANK)).astype(BF16)
        o = jnp.dot(o_lat, wuv_ref[h], preferred_element_type=F32)
        g = gate_ref[:, h * HEAD_DIM:(h + 1) * HEAD_DIM]
        o_ref[:, h * HEAD_DIM:(h + 1) * HEAD_DIM] = (o * _silu(g)).astype(o_ref.dtype)


def _dsa_attention(qidx, widx, kidx, qlat, ckv, wuv, gate, qb=128, kt=512):
    nh, s, _ = qidx.shape
    topk = min(TOPK_MAX, s // 4)
    kt = min(kt, s)
    nkt_max = s // kt
    slopes = tuple(float(v) * math.log2(math.e) for v in _alibi_slopes())
    kern = functools.partial(_dsa_attn_kernel, qb=qb, kt=kt, nkt_max=nkt_max, topk=topk,
                             slopes=slopes)
    return pl.pallas_call(
        kern,
        grid=(s // qb,),
        in_specs=[
            pl.BlockSpec((nh, qb, IDX_DIM), lambda i: (0, i, 0)),
            pl.BlockSpec((qb, LANES), lambda i: (i, 0)),
            pl.BlockSpec((s, IDX_DIM), lambda i: (0, 0)),
            pl.BlockSpec((nh, qb, KV_RANK), lambda i: (0, i, 0)),
            pl.BlockSpec((s, KV_RANK), lambda i: (0, 0)),
            pl.BlockSpec((nh, KV_RANK, HEAD_DIM), lambda i: (0, 0, 0)),
            pl.BlockSpec((qb, BRANCH), lambda i: (i, 0)),
        ],
        out_specs=pl.BlockSpec((qb, BRANCH), lambda i: (i, 0)),
        out_shape=jax.ShapeDtypeStruct((s, BRANCH), BF16),
        scratch_shapes=[
            pltpu.VMEM((nkt_max, qb, kt), jnp.int32),
            pltpu.VMEM((nh, qb, LANES), F32),
            pltpu.VMEM((nh * qb, kt), F32),
            pltpu.VMEM((nh * qb, kt), F32),
            pltpu.VMEM((nh, qb, LANES), F32),
            pltpu.VMEM((nh, qb, LANES), F32),
            pltpu.VMEM((nh, qb, kt), BF16),
            pltpu.VMEM((nh, qb, LANES), F32),
            pltpu.VMEM((nh, qb, LANES), F32),
            pltpu.VMEM((nh, qb, LANES), F32),
            pltpu.VMEM((nh, qb, KV_RANK), F32),
        ],
        compiler_params=_cparams(("arbitrary",), V7X_VMEM_LIMIT_BYTES),
        name="dsa_attention",
    )(qidx, widx, kidx, qlat, ckv, wuv, gate)


MOBA_POS_SPLIT = 64
MOBA_SEL_LANE0 = 8
MOBA_MASK = -(2.0 ** 100)


def _moba_key_consts(s):
    pos = np.arange(s)
    kc = np.zeros((s, LANES), np.float32)
    kc[:, 0:3] = (MOBA_POS_SPLIT * (pos // MOBA_POS_SPLIT))[:, None]
    kc[:, 3:6] = (pos % MOBA_POS_SPLIT)[:, None]
    kc[pos, MOBA_SEL_LANE0 + pos // MOBA_BLOCK] = 1.0
    return jnp.asarray(kc, BF16)


def _moba_query_consts():
    bf = lambda a: a.astype(ml_dtypes.bfloat16).astype(np.float32)
    v = (_alibi_slopes() * np.float32(math.log2(math.e))).astype(np.float32)
    hi = bf(v)
    mid = bf(v - hi)
    lo = bf(v - hi - mid)
    qc = np.zeros((N_HEADS, 1, LANES), np.float32)
    for p, piece in enumerate((hi, mid, lo)):
        qc[:, 0, p] = piece
        qc[:, 0, 3 + p] = piece
    return jnp.asarray(qc)


def _moba_attn_kernel(qc_ref, q_ref, k_ref, v_ref, kc_ref, gate_ref, o_ref,
                      kaug_scr, kmean_scr, m_scr, l_scr, acc_scr, *, nb, qt, kt):
    blk = MOBA_BLOCK
    i = pl.program_id(1)
    sub = qt // blk
    tiles = qt // kt

    @pl.when(i == 0)
    def _():
        kaug_scr[:, 0:HEAD_DIM] = k_ref[...]
        kaug_scr[:, HEAD_DIM:] = kc_ref[...]
        kmean_scr[...] = jnp.zeros(kmean_scr.shape, F32)
        for j in range(nb):
            kj = k_ref[j * blk:(j + 1) * blk, :].astype(F32)
            r = MOBA_SEL_LANE0 + j
            kmean_scr[r:r + 1, :] = jnp.sum(kj, axis=0, keepdims=True) * (1.0 / blk)

    q = q_ref[...]
    km = kmean_scr[...]
    km_hi = km.astype(BF16)
    km_lo = (km - km_hi.astype(F32)).astype(BF16)
    g = _nt_dot(q, km_hi) + _nt_dot(q, km_lo)
    lane = lax.broadcasted_iota(jnp.int32, (qt, LANES), 1)
    lane_f = lane.astype(F32)
    blk_id = lane - MOBA_SEL_LANE0
    own = i * sub + lax.broadcasted_iota(jnp.int32, (qt, LANES), 0) // blk
    past = jnp.logical_and(blk_id >= 0, blk_id < own)
    g = jnp.where(past, g, -jnp.inf)
    selb = jnp.full((qt, LANES), MOBA_MASK, F32)
    for _ in range(MOBA_TOPK):
        mx = jnp.max(g, axis=1, keepdims=True)
        first = jnp.min(jnp.where(g == mx, lane_f, float(LANES)), axis=1, keepdims=True)
        pick = lane_f == first
        selb = jnp.where(pick, 0.0, selb)
        g = jnp.where(pick, -jnp.inf, g)
    selb = jnp.where(past, selb, MOBA_MASK)
    selb = jnp.where(blk_id == own, 0.0, selb)
    aug = jnp.where(lane < MOBA_SEL_LANE0, qc_ref[...], selb)
    q_aug = jnp.concatenate([q, aug.astype(BF16)], axis=1)

    m_scr[...] = jnp.full(m_scr.shape, MOBA_MASK, F32)
    l_scr[...] = jnp.zeros(l_scr.shape, F32)
    acc_scr[...] = jnp.zeros(acc_scr.shape, F32)

    def tile(c0, causal):
        s = _nt_dot(q_aug, kaug_scr[pl.ds(c0, kt), :])
        if causal:
            qpos = i * qt + lax.broadcasted_iota(jnp.int32, (qt, kt), 0)
            kpos = c0 + lax.broadcasted_iota(jnp.int32, (qt, kt), 1)
            s = jnp.where(kpos <= qpos, s, MOBA_MASK)
        m_prev = m_scr[...]
        m_next = jnp.maximum(m_prev, jnp.max(s, axis=1, keepdims=True))
        alpha = jnp.exp2(m_prev - m_next)
        p = jnp.exp2(s - _tile_lanes(m_next, kt))
        l_scr[...] = alpha * l_scr[...] + jnp.sum(p, axis=1, keepdims=True)
        m_scr[...] = m_next
        acc_scr[...] = acc_scr[...] * alpha + jnp.dot(
            p.astype(BF16), v_ref[pl.ds(c0, kt), :], preferred_element_type=F32)

    def past_body(j, carry):
        tile(pl.multiple_of(j * kt, kt), False)
        return carry

    lax.fori_loop(0, i * tiles, past_body, 0)
    for d in range(tiles):
        tile(pl.multiple_of((i * tiles + d) * kt, kt), True)

    o = acc_scr[...] / l_scr[...]
    o_ref[...] = (o * _silu(gate_ref[...])).astype(o_ref.dtype)


def _moba_attention(qkv, gate, qt=1024, kt=512):
    s = qkv.shape[0]
    nb = s // MOBA_BLOCK
    qt = min(qt, s)
    kt = min(kt, qt)
    assert MOBA_SEL_LANE0 + nb <= LANES and s < MOBA_POS_SPLIT * 256
    assert s % qt == 0 and qt % kt == 0 and kt % MOBA_BLOCK == 0
    kern = functools.partial(_moba_attn_kernel, nb=nb, qt=qt, kt=kt)
    return pl.pallas_call(
        kern,
        grid=(N_HEADS, s // qt),
        in_specs=[
            pl.BlockSpec((None, 1, LANES), lambda h, i: (h, 0, 0)),
            pl.BlockSpec((qt, HEAD_DIM), lambda h, i: (i, h)),
            pl.BlockSpec((s, HEAD_DIM), lambda h, i: (0, N_HEADS + h)),
            pl.BlockSpec((s, HEAD_DIM), lambda h, i: (0, 2 * N_HEADS + h)),
            pl.BlockSpec((s, LANES), lambda h, i: (0, 0)),
            pl.BlockSpec((qt, HEAD_DIM), lambda h, i: (i, h)),
        ],
        out_specs=pl.BlockSpec((qt, HEAD_DIM), lambda h, i: (i, h)),
        out_shape=jax.ShapeDtypeStruct((s, BRANCH), BF16),
        scratch_shapes=[
            pltpu.VMEM((s, 2 * HEAD_DIM), BF16),
            pltpu.VMEM((LANES, HEAD_DIM), F32),
            pltpu.VMEM((qt, LANES), F32),
            pltpu.VMEM((qt, LANES), F32),
            pltpu.VMEM((qt, HEAD_DIM), F32),
        ],
        compiler_params=_cparams(("arbitrary", "arbitrary"), V7X_VMEM_LIMIT_BYTES),
        name="moba_attention",
    )(_moba_query_consts(), qkv, qkv, qkv, _moba_key_consts(s), gate)


def _out_ln_kernel(og_ref, wo_ref, x_ref, gmod_ref, lng_ref, lnb_ref, scl_ref, shift_ref,
                   xo_ref, ho_ref, *, alpha):
    y = jnp.dot(og_ref[...], wo_ref[...], preferred_element_type=F32)
    z = alpha * x_ref[...] + gmod_ref[...] * y
    mu = jnp.mean(z, axis=-1, keepdims=True)
    zc = z - mu
    var = jnp.mean(zc * zc, axis=-1, keepdims=True)
    xn = zc * lax.rsqrt(var + LN_EPS) * lng_ref[...] + lnb_ref[...]
    xo_ref[...] = xn
    ho_ref[...] = (xn * (1.0 + scl_ref[...]) + shift_ref[...]).astype(ho_ref.dtype)


def _out_proj_ln(og, wo, x, gmod, lng, lnb, scl_next, shift_next, alpha, tm=256):
    s, d = x.shape
    vec = pl.BlockSpec((1, d), lambda i: (0, 0))
    row = pl.BlockSpec((tm, d), lambda i: (i, 0))
    kern = functools.partial(_out_ln_kernel, alpha=alpha)
    return pl.pallas_call(
        kern,
        grid=(s // tm,),
        in_specs=[row, pl.BlockSpec((d, d), lambda i: (0, 0)), row, vec, vec, vec, vec, vec],
        out_specs=[row, row],
        out_shape=[jax.ShapeDtypeStruct((s, d), F32), jax.ShapeDtypeStruct((s, d), BF16)],
        compiler_params=_cparams(("arbitrary",), V7X_VMEM_LIMIT_BYTES),
        name="out_proj_ln",
    )(og, wo, x, gmod, lng, lnb, scl_next, shift_next)


def kernel(x, c, ada_w, ada_b, ln_g, ln_b, dsa_w_in, dsa_g_q, dsa_g_kv, dsa_w_uq, dsa_w_qi,
           dsa_w_uk, dsa_w_uv, dsa_w_o, moba_w_in, moba_w_o):
    batch, s, d = x.shape
    assert batch == 1 and d == BRANCH
    depth = ada_w.shape[0]
    alpha = float((2 * depth) ** 0.25)
    scale = HEAD_DIM ** -0.5

    mod = _adaln_mod(c, ada_w, ada_b)
    shift = lambda i: mod[i, :, 0:d]
    scl = lambda i: mod[i, :, d:2 * d]
    gmod = lambda i: mod[i, :, 2 * d:3 * d]

    xs = x.reshape(s, d)
    h = _modulate(xs, scl(0), shift(0))

    o0, o1, o2, o3 = Q_RANK, Q_RANK + KV_RANK, Q_RANK + KV_RANK + IDX_DIM, \
        Q_RANK + KV_RANK + IDX_DIM + IDX_HEADS
    for i in range(depth):
        j = i // 2
        if i % 2 == 0:
            w_in = dsa_w_in[j]
            wq = w_in[:, :o0].astype(BF16)
            wkv = w_in[:, o0:o1].astype(BF16)
            wki = w_in[:, o1:o2].astype(BF16)
            wwi = jnp.pad(w_in[:, o2:o3] * (IDX_HEADS ** -0.5),
                          ((0, 0), (0, LANES - IDX_HEADS))).astype(BF16)
            wg = w_in[:, o3:].astype(BF16)
            wqi_h = (dsa_w_qi[j] * (IDX_DIM ** -0.5)).reshape(Q_RANK, IDX_HEADS, IDX_DIM)
            wqi_h = wqi_h.transpose(1, 0, 2).astype(BF16)
            wuq_h = dsa_w_uq[j].reshape(Q_RANK, N_HEADS, HEAD_DIM).transpose(1, 0, 2).astype(BF16)
            wuk_h = (dsa_w_uk[j] * (scale * math.log2(math.e))).astype(BF16)
            wuv_h = dsa_w_uv[j].astype(BF16)
            wo = dsa_w_o[j].astype(BF16)

            cq, ckv, kidx, widx = _dsa_latent(h, wq, wkv, wki, wwi,
                                              dsa_g_q[j].reshape(1, -1), dsa_g_kv[j].reshape(1, -1))
            gate = _matmul(h, wg, F32, name="dsa_gate")
            qidx, qlat = _dsa_query(cq, wqi_h, wuq_h, wuk_h)
            og = _dsa_attention(qidx, widx, kidx, qlat, ckv, wuv_h, gate)
        else:
            w_in = moba_w_in[j]
            wqkv = jnp.concatenate([w_in[:, :BRANCH] * (scale * math.log2(math.e)),
                                    w_in[:, BRANCH:3 * BRANCH]], axis=1).astype(BF16)
            wg = w_in[:, 3 * BRANCH:].astype(BF16)
            wo = moba_w_o[j].astype(BF16)
            qkv = _matmul(h, wqkv, BF16, name="moba_qkv")
            gate = _matmul(h, wg, F32, name="moba_gate")
            og = _moba_attention(qkv, gate)
        nxt = min(i + 1, depth - 1)
        xs, h = _out_proj_ln(og, wo, xs, gmod(i), ln_g[i].reshape(1, d), ln_b[i].reshape(1, d),
                             scl(nxt), shift(nxt), alpha)
    return xs.reshape(batch, s, d)
```

```python
import functools
import math

import jax
import jax.numpy as jnp
import ml_dtypes
import numpy as np
from jax import lax
from jax.experimental import pallas as pl
from jax.experimental.pallas import tpu as pltpu

N_HEADS = 16
HEAD_DIM = 128
BRANCH = N_HEADS * HEAD_DIM
Q_RANK = 512
KV_RANK = 256
IDX_HEADS = 16
IDX_DIM = 128
TOPK_MAX = 256
MOBA_BLOCK = 256
MOBA_TOPK = 3
LN_EPS = 1e-5

LANES = 128
V7X_VMEM_LIMIT_BYTES = 56 * 1024 * 1024

NEG_BIG = -1e30
INT_MIN = -(2 ** 31)

F32 = jnp.float32
BF16 = jnp.bfloat16


def _alibi_slopes():
    return np.asarray(2.0 ** (-8.0 * np.arange(1, N_HEADS + 1) / N_HEADS), dtype=np.float32)


def _silu(v):
    return v / (1.0 + jnp.exp(-v))


def _nt_dot(a, b):
    return lax.dot_general(a, b, (((1,), (1,)), ((), ())), preferred_element_type=F32)


def _tile_lanes(v, width):
    reps = width // LANES
    if reps == 1:
        return v
    return jnp.concatenate([v] * reps, axis=1)


def _cparams(sem, vmem=None):
    return pltpu.CompilerParams(dimension_semantics=sem, vmem_limit_bytes=vmem)


def _adaln_kernel(c_ref, w_ref, b_ref, o_ref):
    c_act = _silu(c_ref[...])
    prod = w_ref[...] * c_act
    o_ref[...] = jnp.sum(prod, axis=0, keepdims=True) + b_ref[...]


def _adaln_mod(c, ada_w, ada_b, tn=512):
    depth, d, n = ada_w.shape
    c_col = c.reshape(d, 1)
    b3 = ada_b.reshape(depth, 1, n)
    out = pl.pallas_call(
        _adaln_kernel,
        grid=(depth, n // tn),
        in_specs=[
            pl.BlockSpec((d, 1), lambda i, j: (0, 0)),
            pl.BlockSpec((None, d, tn), lambda i, j: (i, 0, j)),
            pl.BlockSpec((None, 1, tn), lambda i, j: (i, 0, j)),
        ],
        out_specs=pl.BlockSpec((None, 1, tn), lambda i, j: (i, 0, j)),
        out_shape=jax.ShapeDtypeStruct((depth, 1, n), F32),
        compiler_params=_cparams(("arbitrary", "arbitrary")),
        name="adaln_mod",
    )(c_col, ada_w, b3)
    return out


def _modulate_kernel(x_ref, scl_ref, shift_ref, o_ref):
    o_ref[...] = (x_ref[...] * (1.0 + scl_ref[...]) + shift_ref[...]).astype(o_ref.dtype)


def _modulate(x, scl, shift, tm=512):
    s, d = x.shape
    return pl.pallas_call(
        _modulate_kernel,
        grid=(s // tm,),
        in_specs=[
            pl.BlockSpec((tm, d), lambda i: (i, 0)),
            pl.BlockSpec((1, d), lambda i: (0, 0)),
            pl.BlockSpec((1, d), lambda i: (0, 0)),
        ],
        out_specs=pl.BlockSpec((tm, d), lambda i: (i, 0)),
        out_shape=jax.ShapeDtypeStruct((s, d), BF16),
        compiler_params=_cparams(("arbitrary",)),
        name="modulate",
    )(x, scl, shift)


def _mm_kernel(x_ref, w_ref, o_ref):
    o_ref[...] = jnp.dot(x_ref[...], w_ref[...], preferred_element_type=F32).astype(o_ref.dtype)


def _matmul(x, w, out_dtype, tm=1024, tn=1024, name="matmul"):
    m, k = x.shape
    _, n = w.shape
    tm = min(tm, m)
    tn = min(tn, n)
    return pl.pallas_call(
        _mm_kernel,
        grid=(m // tm, n // tn),
        in_specs=[
            pl.BlockSpec((tm, k), lambda i, j: (i, 0)),
            pl.BlockSpec((k, tn), lambda i, j: (0, j)),
        ],
        out_specs=pl.BlockSpec((tm, tn), lambda i, j: (i, j)),
        out_shape=jax.ShapeDtypeStruct((m, n), out_dtype),
        compiler_params=_cparams(("arbitrary", "arbitrary"), V7X_VMEM_LIMIT_BYTES),
        name=name,
    )(x, w)


def _rms(v, g):
    return v * lax.rsqrt(jnp.mean(v * v, axis=-1, keepdims=True) + LN_EPS) * g


def _dsa_latent_kernel(x_ref, wq_ref, wkv_ref, wki_ref, wwi_ref, gq_ref, gkv_ref,
                       cq_ref, ckv_ref, kidx_ref, widx_ref):
    x = x_ref[...]
    cq = jnp.dot(x, wq_ref[...], preferred_element_type=F32)
    cq_ref[...] = _rms(cq, gq_ref[...]).astype(cq_ref.dtype)
    ckv = jnp.dot(x, wkv_ref[...], preferred_element_type=F32)
    ckv_ref[...] = _rms(ckv, gkv_ref[...]).astype(ckv_ref.dtype)
    kidx_ref[...] = jnp.dot(x, wki_ref[...], preferred_element_type=F32).astype(kidx_ref.dtype)
    widx_ref[...] = jnp.dot(x, wwi_ref[...], preferred_element_type=F32)


def _dsa_latent(h, wq, wkv, wki, wwi, gq, gkv, tm=512):
    s, d = h.shape
    full = lambda shape: pl.BlockSpec(shape, lambda i: (0, 0))
    row = lambda n: pl.BlockSpec((tm, n), lambda i: (i, 0))
    return pl.pallas_call(
        _dsa_latent_kernel,
        grid=(s // tm,),
        in_specs=[row(d), full((d, Q_RANK)), full((d, KV_RANK)), full((d, IDX_DIM)),
                  full((d, LANES)), full((1, Q_RANK)), full((1, KV_RANK))],
        out_specs=[row(Q_RANK), row(KV_RANK), row(IDX_DIM), row(LANES)],
        out_shape=[jax.ShapeDtypeStruct((s, Q_RANK), BF16),
                   jax.ShapeDtypeStruct((s, KV_RANK), BF16),
                   jax.ShapeDtypeStruct((s, IDX_DIM), BF16),
                   jax.ShapeDtypeStruct((s, LANES), F32)],
        compiler_params=_cparams(("arbitrary",), V7X_VMEM_LIMIT_BYTES),
        name="dsa_latent",
    )(h, wq, wkv, wki, wwi, gq, gkv)


def _dsa_query_kernel(cq_ref, wqi_ref, wuq_ref, wuk_ref, qidx_ref, qlat_ref, *, hp):
    cq = cq_ref[...]
    qi = jnp.dot(cq, wqi_ref[...], preferred_element_type=F32)
    q = jnp.dot(cq, wuq_ref[...], preferred_element_type=F32).astype(BF16)
    for k in range(hp):
        qidx_ref[k] = qi[:, k * IDX_DIM:(k + 1) * IDX_DIM].astype(qidx_ref.dtype)
        qlat_ref[k] = jnp.dot(q[:, k * HEAD_DIM:(k + 1) * HEAD_DIM], wuk_ref[k],
                              preferred_element_type=F32).astype(qlat_ref.dtype)


def _dsa_query(cq, wqi, wuq, wuk_h, tm=1024, hp=4):
    s, r = cq.shape
    tm = min(tm, s)
    return pl.pallas_call(
        functools.partial(_dsa_query_kernel, hp=hp),
        grid=(s // tm, N_HEADS // hp),
        in_specs=[
            pl.BlockSpec((tm, r), lambda i, g: (i, 0)),
            pl.BlockSpec((r, hp * IDX_DIM), lambda i, g: (0, g)),
            pl.BlockSpec((r, hp * HEAD_DIM), lambda i, g: (0, g)),
            pl.BlockSpec((hp, HEAD_DIM, KV_RANK), lambda i, g: (g, 0, 0)),
        ],
        out_specs=[
            pl.BlockSpec((hp, tm, IDX_DIM), lambda i, g: (g, i, 0)),
            pl.BlockSpec((hp, tm, KV_RANK), lambda i, g: (g, i, 0)),
        ],
        out_shape=[jax.ShapeDtypeStruct((N_HEADS, s, IDX_DIM), BF16),
                   jax.ShapeDtypeStruct((N_HEADS, s, KV_RANK), BF16)],
        compiler_params=_cparams(("arbitrary", "arbitrary"), V7X_VMEM_LIMIT_BYTES),
        name="dsa_query",
    )(cq, wqi, wuq, wuk_h)


def _dsa_attn_kernel(qidx_ref, widx_ref, kidx_ref, qlat_ref, ckv_ref, wuv_ref, gate_ref,
                     o_ref, key_scr, wb_scr, lga_scr, lgb_scr, mta_scr, mtb_scr, p_scr, m_scr, l_scr,
                     a_scr, acc_scr,
                     *, qb, kt, nkt_max, topk, slopes):
    nh = N_HEADS
    i = pl.program_id(0)
    t0 = i * qb
    nkt = (t0 + qb + kt - 1) // kt
    row = t0 + lax.broadcasted_iota(jnp.int32, (qb, kt), 0)
    lane = lax.broadcasted_iota(jnp.int32, (qb, kt), 1)

    w = widx_ref[...]
    for h in range(nh):
        wb_scr[h] = jnp.broadcast_to(w[:, h:h + 1], (qb, LANES))

    qi = qidx_ref[...].reshape(nh * qb, IDX_DIM)

    def idx_body(j, carry):
        c0 = pl.multiple_of(j * kt, kt)
        res = _nt_dot(qi, kidx_ref[pl.ds(c0, kt), :])
        sc = jnp.zeros((qb, kt), F32)
        for h in range(nh):
            sc = sc + _tile_lanes(wb_scr[h], kt) * jnp.maximum(res[h * qb:(h + 1) * qb], 0.0)
        bits = lax.bitcast_convert_type(sc, jnp.int32)
        key = bits ^ ((bits >> 31) & jnp.int32(0x7FFFFFFF))
        key = jnp.where(c0 + lane <= row, key, jnp.int32(INT_MIN))
        key_scr[j] = key
        return carry

    lax.fori_loop(0, nkt, idx_body, 0)

    def bit_body(b, cur):
        cand = cur ^ lax.shift_left(jnp.int32(1), jnp.int32(31) - b)

        def cnt_body(j, cnt):
            tile = key_scr[j]
            for c in range(kt // LANES):
                cnt = cnt + jnp.where(tile[:, c * LANES:(c + 1) * LANES] >= cand, 1.0, 0.0)
            return cnt

        cnt = lax.fori_loop(0, nkt, cnt_body, jnp.zeros((qb, LANES), F32))
        tot = jnp.sum(cnt, axis=1, keepdims=True)
        return jnp.where(tot >= float(topk), cand, cur)

    thr = lax.fori_loop(0, 32, bit_body, jnp.full((qb, LANES), INT_MIN, jnp.int32))
    thr_t = _tile_lanes(thr, kt)

    ql = qlat_ref[...].reshape(nh * qb, KV_RANK)
    m_scr[...] = jnp.full(m_scr.shape, NEG_BIG, F32)
    l_scr[...] = jnp.zeros(l_scr.shape, F32)
    acc_scr[...] = jnp.zeros(acc_scr.shape, F32)

    def tile_start(j):
        return pl.multiple_of(jnp.minimum(j, nkt_max - 1) * kt, kt)

    def logits_into(j, dst, mt_dst):
        col = j * kt + lane
        sel = jnp.logical_and(key_scr[jnp.minimum(j, nkt_max - 1)] >= thr_t, col <= row)
        bias = jnp.where(sel, 0.0, NEG_BIG)
        colf = (j * kt - t0 + lax.broadcasted_iota(jnp.int32, (1, kt), 1)).astype(F32)
        lg = _nt_dot(ql, ckv_ref[pl.ds(tile_start(j), kt), :])
        for h in range(nh):
            s = lg[h * qb:(h + 1) * qb, :] + slopes[h] * colf + bias
            dst[h * qb:(h + 1) * qb, :] = s
            mt_dst[h] = jnp.broadcast_to(jnp.max(s, axis=1, keepdims=True), (qb, LANES))

    def softmax_pv(j, src, mt_src):
        for h in range(nh):
            s = src[h * qb:(h + 1) * qb, :]
            m_prev = m_scr[h]
            m_next = jnp.maximum(m_prev, mt_src[h])
            alpha = jnp.exp2(m_prev - m_next)
            p = jnp.exp2(s - _tile_lanes(m_next, kt))
            l_scr[h] = alpha * l_scr[h] + jnp.sum(p, axis=1, keepdims=True)
            m_scr[h] = m_next
            a_scr[h] = alpha
            p_scr[h] = p.astype(BF16)
        pv = jnp.dot(p_scr[...].reshape(nh * qb, kt), ckv_ref[pl.ds(tile_start(j), kt), :],
                     preferred_element_type=F32)
        a = _tile_lanes(a_scr[...].reshape(nh * qb, LANES), KV_RANK)
        acc_scr[...] = (acc_scr[...].reshape(nh * qb, KV_RANK) * a + pv).reshape(nh, qb, KV_RANK)

    def pair_body(pp, carry):
        j = 2 * pp
        logits_into(j + 1, lgb_scr, mtb_scr)
        softmax_pv(j, lga_scr, mta_scr)
        logits_into(j + 2, lga_scr, mta_scr)
        softmax_pv(j + 1, lgb_scr, mtb_scr)
        return carry

    logits_into(0, lga_scr, mta_scr)
    lax.fori_loop(0, (nkt + 1) // 2, pair_body, 0)

    for h in range(nh):
        o_lat = (acc_scr[h] / _tile_lanes(l_scr[h], KV_RANK)).astype(BF16)
        o = jnp.dot(o_lat, wuv_ref[h], preferred_element_type=F32)
        g = gate_ref[:, h * HEAD_DIM:(h + 1) * HEAD_DIM]
        o_ref[:, h * HEAD_DIM:(h + 1) * HEAD_DIM] = (o * _silu(g)).astype(o_ref.dtype)


def _dsa_attention(qidx, widx, kidx, qlat, ckv, wuv, gate, qb=128, kt=512):
    nh, s, _ = qidx.shape
    topk = min(TOPK_MAX, s // 4)
    kt = min(kt, s)
    nkt_max = s // kt
    slopes = tuple(float(v) * math.log2(math.e) for v in _alibi_slopes())
    kern = functools.partial(_dsa_attn_kernel, qb=qb, kt=kt, nkt_max=nkt_max, topk=topk,
                             slopes=slopes)
    return pl.pallas_call(
        kern,
        grid=(s // qb,),
        in_specs=[
            pl.BlockSpec((nh, qb, IDX_DIM), lambda i: (0, i, 0)),
            pl.BlockSpec((qb, LANES), lambda i: (i, 0)),
            pl.BlockSpec((s, IDX_DIM), lambda i: (0, 0)),
            pl.BlockSpec((nh, qb, KV_RANK), lambda i: (0, i, 0)),
            pl.BlockSpec((s, KV_RANK), lambda i: (0, 0)),
            pl.BlockSpec((nh, KV_RANK, HEAD_DIM), lambda i: (0, 0, 0)),
            pl.BlockSpec((qb, BRANCH), lambda i: (i, 0)),
        ],
        out_specs=pl.BlockSpec((qb, BRANCH), lambda i: (i, 0)),
        out_shape=jax.ShapeDtypeStruct((s, BRANCH), BF16),
        scratch_shapes=[
            pltpu.VMEM((nkt_max, qb, kt), jnp.int32),
            pltpu.VMEM((nh, qb, LANES), F32),
            pltpu.VMEM((nh * qb, kt), F32),
            pltpu.VMEM((nh * qb, kt), F32),
            pltpu.VMEM((nh, qb, LANES), F32),
            pltpu.VMEM((nh, qb, LANES), F32),
            pltpu.VMEM((nh, qb, kt), BF16),
            pltpu.VMEM((nh, qb, LANES), F32),
            pltpu.VMEM((nh, qb, LANES), F32),
            pltpu.VMEM((nh, qb, LANES), F32),
            pltpu.VMEM((nh, qb, KV_RANK), F32),
        ],
        compiler_params=_cparams(("arbitrary",), V7X_VMEM_LIMIT_BYTES),
        name="dsa_attention",
    )(qidx, widx, kidx, qlat, ckv, wuv, gate)


MOBA_POS_SPLIT = 64
MOBA_SEL_LANE0 = 8
MOBA_MASK = -(2.0 ** 100)


def _moba_key_consts(s):
    pos = np.arange(s)
    kc = np.zeros((s, LANES), np.float32)
    kc[:, 0:3] = (MOBA_POS_SPLIT * (pos // MOBA_POS_SPLIT))[:, None]
    kc[:, 3:6] = (pos % MOBA_POS_SPLIT)[:, None]
    kc[pos, MOBA_SEL_LANE0 + pos // MOBA_BLOCK] = 1.0
    return jnp.asarray(kc, BF16)


def _moba_query_consts():
    bf = lambda a: a.astype(ml_dtypes.bfloat16).astype(np.float32)
    v = (_alibi_slopes() * np.float32(math.log2(math.e))).astype(np.float32)
    hi = bf(v)
    mid = bf(v - hi)
    lo = bf(v - hi - mid)
    qc = np.zeros((N_HEADS, 1, LANES), np.float32)
    for p, piece in enumerate((hi, mid, lo)):
        qc[:, 0, p] = piece
        qc[:, 0, 3 + p] = piece
    return jnp.asarray(qc)


def _moba_attn_kernel(qc_ref, q_ref, k_ref, v_ref, kc_ref, gate_ref, o_ref,
                      kaug_scr, kmean_scr, sa_scr, sb_scr, mta_scr, mtb_scr, m_scr, l_scr, acc_scr,
                      *, nb, qt, kt):
    blk = MOBA_BLOCK
    i = pl.program_id(1)
    sub = qt // blk

    @pl.when(i == 0)
    def _():
        kaug_scr[:, 0:HEAD_DIM] = k_ref[...]
        kaug_scr[:, HEAD_DIM:] = kc_ref[...]
        kmean_scr[...] = jnp.zeros(kmean_scr.shape, F32)
        for j in range(nb):
            kj = k_ref[j * blk:(j + 1) * blk, :].astype(F32)
            r = MOBA_SEL_LANE0 + j
            kmean_scr[r:r + 1, :] = jnp.sum(kj, axis=0, keepdims=True) * (1.0 / blk)

    q = q_ref[...]
    km = kmean_scr[...]
    km_hi = km.astype(BF16)
    km_lo = (km - km_hi.astype(F32)).astype(BF16)
    g = _nt_dot(q, km_hi) + _nt_dot(q, km_lo)
    lane = lax.broadcasted_iota(jnp.int32, (qt, LANES), 1)
    lane_f = lane.astype(F32)
    blk_id = lane - MOBA_SEL_LANE0
    own = i * sub + lax.broadcasted_iota(jnp.int32, (qt, LANES), 0) // blk
    past = jnp.logical_and(blk_id >= 0, blk_id < own)
    g = jnp.where(past, g, -jnp.inf)
    selb = jnp.full((qt, LANES), MOBA_MASK, F32)
    for _ in range(MOBA_TOPK):
        mx = jnp.max(g, axis=1, keepdims=True)
        first = jnp.min(jnp.where(g == mx, lane_f, float(LANES)), axis=1, keepdims=True)
        pick = lane_f == first
        selb = jnp.where(pick, 0.0, selb)
        g = jnp.where(pick, -jnp.inf, g)
    selb = jnp.where(past, selb, MOBA_MASK)
    selb = jnp.where(blk_id == own, 0.0, selb)
    aug = jnp.where(lane < MOBA_SEL_LANE0, qc_ref[...], selb)
    q_aug = jnp.concatenate([q, aug.astype(BF16)], axis=1)

    m_scr[...] = jnp.full(m_scr.shape, MOBA_MASK, F32)
    l_scr[...] = jnp.zeros(l_scr.shape, F32)
    acc_scr[...] = jnp.zeros(acc_scr.shape, F32)

    def scores_into(j, dst, mt_dst, causal):
        c0 = pl.multiple_of(j * kt, kt)
        s = _nt_dot(q_aug, kaug_scr[pl.ds(c0, kt), :])
        if causal:
            qpos = i * qt + lax.broadcasted_iota(jnp.int32, (qt, kt), 0)
            kpos = c0 + lax.broadcasted_iota(jnp.int32, (qt, kt), 1)
            s = jnp.where(kpos <= qpos, s, MOBA_MASK)
        dst[...] = s
        mt_dst[...] = jnp.broadcast_to(jnp.max(s, axis=1, keepdims=True), (qt, LANES))

    def softmax_pv(j, src, mt_src):
        c0 = pl.multiple_of(j * kt, kt)
        m_prev = m_scr[...]
        m_next = jnp.maximum(m_prev, mt_src[...])
        alpha = jnp.exp2(m_prev - m_next)
        p = jnp.exp2(src[...] - _tile_lanes(m_next, kt))
        l_scr[...] = alpha * l_scr[...] + jnp.sum(p, axis=1, keepdims=True)
        m_scr[...] = m_next
        acc_scr[...] = acc_scr[...] * alpha + jnp.dot(
            p.astype(BF16), v_ref[pl.ds(c0, kt), :], preferred_element_type=F32)

    def pair_body(pp, carry):
        j = 2 * pp
        scores_into(j + 1, sb_scr, mtb_scr, False)
        softmax_pv(j, sa_scr, mta_scr)
        scores_into(j + 2, sa_scr, mta_scr, True)
        softmax_pv(j + 1, sb_scr, mtb_scr)
        return carry

    scores_into(0, sa_scr, mta_scr, True)
    lax.fori_loop(0, i, pair_body, 0)
    scores_into(2 * i + 1, sb_scr, mtb_scr, True)
    softmax_pv(2 * i, sa_scr, mta_scr)
    softmax_pv(2 * i + 1, sb_scr, mtb_scr)

    o = acc_scr[...] / l_scr[...]
    o_ref[...] = (o * _silu(gate_ref[...])).astype(o_ref.dtype)


def _moba_attention(qkv, gate, qt=1024, kt=512):
    s = qkv.shape[0]
    nb = s // MOBA_BLOCK
    qt = min(qt, s)
    kt = min(kt, qt)
    assert MOBA_SEL_LANE0 + nb <= LANES and s < MOBA_POS_SPLIT * 256
    assert s % qt == 0 and qt == 2 * kt and kt % MOBA_BLOCK == 0
    kern = functools.partial(_moba_attn_kernel, nb=nb, qt=qt, kt=kt)
    return pl.pallas_call(
        kern,
        grid=(N_HEADS, s // qt),
        in_specs=[
            pl.BlockSpec((None, 1, LANES), lambda h, i: (h, 0, 0)),
            pl.BlockSpec((qt, HEAD_DIM), lambda h, i: (i, h)),
            pl.BlockSpec((s, HEAD_DIM), lambda h, i: (0, N_HEADS + h)),
            pl.BlockSpec((s, HEAD_DIM), lambda h, i: (0, 2 * N_HEADS + h)),
            pl.BlockSpec((s, LANES), lambda h, i: (0, 0)),
            pl.BlockSpec((qt, HEAD_DIM), lambda h, i: (i, h)),
        ],
        out_specs=pl.BlockSpec((qt, HEAD_DIM), lambda h, i: (i, h)),
        out_shape=jax.ShapeDtypeStruct((s, BRANCH), BF16),
        scratch_shapes=[
            pltpu.VMEM((s, 2 * HEAD_DIM), BF16),
            pltpu.VMEM((LANES, HEAD_DIM), F32),
            pltpu.VMEM((qt, kt), F32),
            pltpu.VMEM((qt, kt), F32),
            pltpu.VMEM((qt, LANES), F32),
            pltpu.VMEM((qt, LANES), F32),
            pltpu.VMEM((qt, LANES), F32),
            pltpu.VMEM((qt, LANES), F32),
            pltpu.VMEM((qt, HEAD_DIM), F32),
        ],
        compiler_params=_cparams(("arbitrary", "arbitrary"), V7X_VMEM_LIMIT_BYTES),
        name="moba_attention",
    )(_moba_query_consts(), qkv, qkv, qkv, _moba_key_consts(s), gate)


def _out_ln_kernel(og_ref, wo_ref, x_ref, gmod_ref, lng_ref, lnb_ref, scl_ref, shift_ref,
                   xo_ref, ho_ref, *, alpha):
    y = jnp.dot(og_ref[...], wo_ref[...], preferred_element_type=F32)
    z = alpha * x_ref[...] + gmod_ref[...] * y
    mu = jnp.mean(z, axis=-1, keepdims=True)
    zc = z - mu
    var = jnp.mean(zc * zc, axis=-1, keepdims=True)
    xn = zc * lax.rsqrt(var + LN_EPS) * lng_ref[...] + lnb_ref[...]
    xo_ref[...] = xn
    ho_ref[...] = (xn * (1.0 + scl_ref[...]) + shift_ref[...]).astype(ho_ref.dtype)


def _out_proj_ln(og, wo, x, gmod, lng, lnb, scl_next, shift_next, alpha, tm=256):
    s, d = x.shape
    vec = pl.BlockSpec((1, d), lambda i: (0, 0))
    row = pl.BlockSpec((tm, d), lambda i: (i, 0))
    kern = functools.partial(_out_ln_kernel, alpha=alpha)
    return pl.pallas_call(
        kern,
        grid=(s // tm,),
        in_specs=[row, pl.BlockSpec((d, d), lambda i: (0, 0)), row, vec, vec, vec, vec, vec],
        out_specs=[row, row],
        out_shape=[jax.ShapeDtypeStruct((s, d), F32), jax.ShapeDtypeStruct((s, d), BF16)],
        compiler_params=_cparams(("arbitrary",), V7X_VMEM_LIMIT_BYTES),
        name="out_proj_ln",
    )(og, wo, x, gmod, lng, lnb, scl_next, shift_next)


def kernel(x, c, ada_w, ada_b, ln_g, ln_b, dsa_w_in, dsa_g_q, dsa_g_kv, dsa_w_uq, dsa_w_qi,
           dsa_w_uk, dsa_w_uv, dsa_w_o, moba_w_in, moba_w_o):
    batch, s, d = x.shape
    assert batch == 1 and d == BRANCH
    depth = ada_w.shape[0]
    alpha = float((2 * depth) ** 0.25)
    scale = HEAD_DIM ** -0.5

    mod = _adaln_mod(c, ada_w, ada_b)
    shift = lambda i: mod[i, :, 0:d]
    scl = lambda i: mod[i, :, d:2 * d]
    gmod = lambda i: mod[i, :, 2 * d:3 * d]

    xs = x.reshape(s, d)
    h = _modulate(xs, scl(0), shift(0))

    o0, o1, o2, o3 = Q_RANK, Q_RANK + KV_RANK, Q_RANK + KV_RANK + IDX_DIM, \
        Q_RANK + KV_RANK + IDX_DIM + IDX_HEADS
    for i in range(depth):
        j = i // 2
        if i % 2 == 0:
            w_in = dsa_w_in[j]
            wq = w_in[:, :o0].astype(BF16)
            wkv = w_in[:, o0:o1].astype(BF16)
            wki = w_in[:, o1:o2].astype(BF16)
            wwi = jnp.pad(w_in[:, o2:o3] * (IDX_HEADS ** -0.5),
                          ((0, 0), (0, LANES - IDX_HEADS))).astype(BF16)
            wg = w_in[:, o3:].astype(BF16)
            wqi = (dsa_w_qi[j] * (IDX_DIM ** -0.5)).astype(BF16)
            wuq = dsa_w_uq[j].astype(BF16)
            wuk_h = (dsa_w_uk[j] * (scale * math.log2(math.e))).astype(BF16)
            wuv_h = dsa_w_uv[j].astype(BF16)
            wo = dsa_w_o[j].astype(BF16)

            cq, ckv, kidx, widx = _dsa_latent(h, wq, wkv, wki, wwi,
                                              dsa_g_q[j].reshape(1, -1), dsa_g_kv[j].reshape(1, -1))
            gate = _matmul(h, wg, F32, name="dsa_gate")
            qidx, qlat = _dsa_query(cq, wqi, wuq, wuk_h)
            og = _dsa_attention(qidx, widx, kidx, qlat, ckv, wuv_h, gate)
        else:
            w_in = moba_w_in[j]
            wqkv = jnp.concatenate([w_in[:, :BRANCH] * (scale * math.log2(math.e)),
                                    w_in[:, BRANCH:3 * BRANCH]], axis=1).astype(BF16)
            wg = w_in[:, 3 * BRANCH:].astype(BF16)
            wo = moba_w_o[j].astype(BF16)
            qkv = _matmul(h, wqkv, BF16, name="moba_qkv")
            gate = _matmul(h, wg, F32, name="moba_gate")
            og = _moba_attention(qkv, gate)
        nxt = min(i + 1, depth - 1)
        xs, h = _out_proj_ln(og, wo, xs, gmod(i), ln_g[i].reshape(1, d), ln_b[i].reshape(1, d),
                             scl(nxt), shift(nxt), alpha)
    return xs.reshape(batch, s, d)
```

```python
import functools
import math

import jax
import jax.numpy as jnp
import ml_dtypes
import numpy as np
from jax import lax
from jax.experimental import pallas as pl
from jax.experimental.pallas import tpu as pltpu

N_HEADS = 16
HEAD_DIM = 128
BRANCH = N_HEADS * HEAD_DIM
Q_RANK = 512
KV_RANK = 256
IDX_HEADS = 16
IDX_DIM = 128
TOPK_MAX = 256
MOBA_BLOCK = 256
MOBA_TOPK = 3
LN_EPS = 1e-5

LANES = 128
V7X_VMEM_LIMIT_BYTES = 56 * 1024 * 1024

NEG_BIG = -1e30
INT_MIN = -(2 ** 31)

F32 = jnp.float32
BF16 = jnp.bfloat16


def _alibi_slopes():
    return np.asarray(2.0 ** (-8.0 * np.arange(1, N_HEADS + 1) / N_HEADS), dtype=np.float32)


def _silu(v):
    return v / (1.0 + jnp.exp(-v))


def _nt_dot(a, b):
    return lax.dot_general(a, b, (((1,), (1,)), ((), ())), preferred_element_type=F32)


def _lanewise(op, x, rep):
    n = x.shape[1] // LANES
    if n == 1:
        return op(x, rep)
    return jnp.concatenate([op(x[:, c * LANES:(c + 1) * LANES], rep) for c in range(n)], axis=1)


def _cparams(sem, vmem=None):
    return pltpu.CompilerParams(dimension_semantics=sem, vmem_limit_bytes=vmem)


def _adaln_kernel(c_ref, w_ref, b_ref, o_ref):
    c_act = _silu(c_ref[...])
    prod = w_ref[...] * c_act
    o_ref[...] = jnp.sum(prod, axis=0, keepdims=True) + b_ref[...]


def _adaln_mod(c, ada_w, ada_b, tn=512):
    depth, d, n = ada_w.shape
    c_col = c.reshape(d, 1)
    b3 = ada_b.reshape(depth, 1, n)
    out = pl.pallas_call(
        _adaln_kernel,
        grid=(depth, n // tn),
        in_specs=[
            pl.BlockSpec((d, 1), lambda i, j: (0, 0)),
            pl.BlockSpec((None, d, tn), lambda i, j: (i, 0, j)),
            pl.BlockSpec((None, 1, tn), lambda i, j: (i, 0, j)),
        ],
        out_specs=pl.BlockSpec((None, 1, tn), lambda i, j: (i, 0, j)),
        out_shape=jax.ShapeDtypeStruct((depth, 1, n), F32),
        compiler_params=_cparams(("arbitrary", "arbitrary")),
        name="adaln_mod",
    )(c_col, ada_w, b3)
    return out


def _modulate_kernel(x_ref, scl_ref, shift_ref, o_ref):
    o_ref[...] = (x_ref[...] * (1.0 + scl_ref[...]) + shift_ref[...]).astype(o_ref.dtype)


def _modulate(x, scl, shift, tm=512):
    s, d = x.shape
    return pl.pallas_call(
        _modulate_kernel,
        grid=(s // tm,),
        in_specs=[
            pl.BlockSpec((tm, d), lambda i: (i, 0)),
            pl.BlockSpec((1, d), lambda i: (0, 0)),
            pl.BlockSpec((1, d), lambda i: (0, 0)),
        ],
        out_specs=pl.BlockSpec((tm, d), lambda i: (i, 0)),
        out_shape=jax.ShapeDtypeStruct((s, d), BF16),
        compiler_params=_cparams(("arbitrary",)),
        name="modulate",
    )(x, scl, shift)


def _mm_kernel(x_ref, w_ref, o_ref):
    o_ref[...] = jnp.dot(x_ref[...], w_ref[...], preferred_element_type=F32).astype(o_ref.dtype)


def _matmul(x, w, out_dtype, tm=1024, tn=1024, name="matmul"):
    m, k = x.shape
    _, n = w.shape
    tm = min(tm, m)
    tn = min(tn, n)
    return pl.pallas_call(
        _mm_kernel,
        grid=(m // tm, n // tn),
        in_specs=[
            pl.BlockSpec((tm, k), lambda i, j: (i, 0)),
            pl.BlockSpec((k, tn), lambda i, j: (0, j)),
        ],
        out_specs=pl.BlockSpec((tm, tn), lambda i, j: (i, j)),
        out_shape=jax.ShapeDtypeStruct((m, n), out_dtype),
        compiler_params=_cparams(("arbitrary", "arbitrary"), V7X_VMEM_LIMIT_BYTES),
        name=name,
    )(x, w)


def _proj_kernel(x_ref, w_ref, o_ref, wb_scr, *, scale, scaled_tiles):
    jn = pl.program_id(0)

    @pl.when(pl.program_id(1) == 0)
    def _():
        w = w_ref[...]
        if scaled_tiles:
            w = w * jnp.where(jn < scaled_tiles, scale, 1.0)
        wb_scr[...] = w.astype(BF16)

    o_ref[...] = jnp.dot(x_ref[...], wb_scr[...], preferred_element_type=F32).astype(o_ref.dtype)


def _proj_stacked(x, w_stack, layer, col0, ncols, out_dtype, scale=1.0, scaled_cols=0,
                  tm=1024, tn=1024, name="proj"):
    m, k = x.shape
    tm = min(tm, m)
    assert col0 % tn == 0 and ncols % tn == 0 and scaled_cols % tn == 0 and m % tm == 0
    c0 = col0 // tn
    kern = functools.partial(_proj_kernel, scale=scale, scaled_tiles=scaled_cols // tn)
    return pl.pallas_call(
        kern,
        grid=(ncols // tn, m // tm),
        in_specs=[
            pl.BlockSpec((tm, k), lambda jn, i: (i, 0)),
            pl.BlockSpec((None, k, tn), lambda jn, i: (layer, 0, c0 + jn)),
        ],
        out_specs=pl.BlockSpec((tm, tn), lambda jn, i: (i, jn)),
        out_shape=jax.ShapeDtypeStruct((m, ncols), out_dtype),
        scratch_shapes=[pltpu.VMEM((k, tn), BF16)],
        compiler_params=_cparams(("arbitrary", "arbitrary"), V7X_VMEM_LIMIT_BYTES),
        name=name,
    )(x, w_stack)


def _rms(v, g):
    return v * lax.rsqrt(jnp.mean(v * v, axis=-1, keepdims=True) + LN_EPS) * g


def _dsa_latent_kernel(x_ref, wq_ref, wkv_ref, wki_ref, wwi_ref, gq_ref, gkv_ref,
                       cq_ref, ckv_ref, ckvt_ref, kidx_ref, widxt_ref):
    x = x_ref[...]
    cq = jnp.dot(x, wq_ref[...], preferred_element_type=F32)
    cq_ref[...] = _rms(cq, gq_ref[...]).astype(cq_ref.dtype)
    ckv = _rms(jnp.dot(x, wkv_ref[...], preferred_element_type=F32), gkv_ref[...])
    ckv_ref[...] = ckv.astype(ckv_ref.dtype)
    ckvt_ref[...] = ckv.T.astype(ckvt_ref.dtype)
    kidx_ref[...] = jnp.dot(x, wki_ref[...], preferred_element_type=F32).astype(kidx_ref.dtype)
    widxt_ref[...] = jnp.dot(x, wwi_ref[...], preferred_element_type=F32).T


def _dsa_latent(h, wq, wkv, wki, wwi, gq, gkv, tm):
    s, d = h.shape
    full = lambda shape: pl.BlockSpec(shape, lambda i: (0, 0))
    row = lambda n: pl.BlockSpec((tm, n), lambda i: (i, 0))
    return pl.pallas_call(
        _dsa_latent_kernel,
        grid=(s // tm,),
        in_specs=[row(d), full((d, Q_RANK)), full((d, KV_RANK)), full((d, IDX_DIM)),
                  full((d, LANES)), full((1, Q_RANK)), full((1, KV_RANK))],
        out_specs=[row(Q_RANK), row(KV_RANK),
                   pl.BlockSpec((None, KV_RANK, tm), lambda i: (i, 0, 0)),
                   row(IDX_DIM),
                   pl.BlockSpec((LANES, tm), lambda i: (0, i))],
        out_shape=[jax.ShapeDtypeStruct((s, Q_RANK), BF16),
                   jax.ShapeDtypeStruct((s, KV_RANK), BF16),
                   jax.ShapeDtypeStruct((s // tm, KV_RANK, tm), BF16),
                   jax.ShapeDtypeStruct((s, IDX_DIM), BF16),
                   jax.ShapeDtypeStruct((LANES, s), F32)],
        compiler_params=_cparams(("arbitrary",), V7X_VMEM_LIMIT_BYTES),
        name="dsa_latent",
    )(h, wq, wkv, wki, wwi, gq, gkv)


def _dsa_query_kernel(cq_ref, wqit_ref, wuqt_ref, wukt_ref, qidxt_ref, qlatt_ref, *, hp):
    cq = cq_ref[...]
    qit = _nt_dot(wqit_ref[...], cq)
    qt = _nt_dot(wuqt_ref[...], cq).astype(BF16)
    for k in range(hp):
        qidxt_ref[k] = qit[k * IDX_DIM:(k + 1) * IDX_DIM, :].astype(qidxt_ref.dtype)
        qlatt_ref[k] = jnp.dot(wukt_ref[k], qt[k * HEAD_DIM:(k + 1) * HEAD_DIM, :],
                               preferred_element_type=F32).astype(qlatt_ref.dtype)


def _dsa_query(cq, wqit, wuqt, wukt_h, tm=1024, hp=4):
    s, r = cq.shape
    tm = min(tm, s)
    return pl.pallas_call(
        functools.partial(_dsa_query_kernel, hp=hp),
        grid=(s // tm, N_HEADS // hp),
        in_specs=[
            pl.BlockSpec((tm, r), lambda i, g: (i, 0)),
            pl.BlockSpec((hp * IDX_DIM, r), lambda i, g: (g, 0)),
            pl.BlockSpec((hp * HEAD_DIM, r), lambda i, g: (g, 0)),
            pl.BlockSpec((hp, KV_RANK, HEAD_DIM), lambda i, g: (g, 0, 0)),
        ],
        out_specs=[
            pl.BlockSpec((hp, IDX_DIM, tm), lambda i, g: (g, 0, i)),
            pl.BlockSpec((hp, KV_RANK, tm), lambda i, g: (g, 0, i)),
        ],
        out_shape=[jax.ShapeDtypeStruct((N_HEADS, IDX_DIM, s), BF16),
                   jax.ShapeDtypeStruct((N_HEADS, KV_RANK, s), BF16)],
        compiler_params=_cparams(("arbitrary", "arbitrary"), V7X_VMEM_LIMIT_BYTES),
        name="dsa_query",
    )(cq, wqit, wuqt, wukt_h)


SUBLANES = 8
DSA_KEY_TILE = 512


def _dsa_attn_kernel(qidxt_ref, widxt_ref, kidx_ref, qlatt_ref, ckv_ref, ckvt_ref, wuvt_ref,
                     gate_ref, o_ref, key_scr, alibi_scr, sa_scr, sb_scr, mta_scr, mtb_scr, p_scr,
                     m_scr, l_scr, a_scr, acc_scr, *, qb, kt, nkt_max, topk, slopes):
    nh = N_HEADS
    i = pl.program_id(0)
    t0 = i * qb
    nkt = (t0 + qb + kt - 1) // kt
    kloc = lax.broadcasted_iota(jnp.int32, (kt, qb), 0)
    qpos = t0 + lax.broadcasted_iota(jnp.int32, (kt, qb), 1)
    head = lambda h: slice(h * qb, (h + 1) * qb)

    @pl.when(i == 0)
    def _():
        r = kloc.astype(F32)
        for h in range(nh):
            alibi_scr[h] = slopes[h] * r

    qi_all = jnp.concatenate([qidxt_ref[h] for h in range(nh)], axis=1)

    def idx_body(j, carry):
        c0 = pl.multiple_of(j * kt, kt)
        res = jnp.dot(kidx_ref[pl.ds(c0, kt), :], qi_all, preferred_element_type=F32)
        sc = jnp.zeros((kt, qb), F32)
        for h in range(nh):
            sc = sc + widxt_ref[h:h + 1, :] * jnp.maximum(res[:, head(h)], 0.0)
        bits = lax.bitcast_convert_type(sc, jnp.int32)
        key = bits ^ ((bits >> 31) & jnp.int32(0x7FFFFFFF))
        key_scr[j] = jnp.where(c0 + kloc <= qpos, key, jnp.int32(INT_MIN))
        return carry

    lax.fori_loop(0, nkt, idx_body, 0)

    ways = 8

    def bit_cond(state):
        b, _, _, active = state
        return jnp.logical_and(b < 32, active > 0)

    def bit_body(state):
        b, cur, ccur, _ = state
        cand = cur ^ lax.shift_left(jnp.int32(1), jnp.int32(31) - b)

        def cnt_body(j, cnt):
            t = key_scr[j].reshape(kt // (ways * SUBLANES), ways, SUBLANES, qb)
            return cnt + jnp.sum(jnp.where(t >= cand[None, None], 1.0, 0.0), axis=0)

        cnt = lax.fori_loop(0, nkt, cnt_body, jnp.zeros((ways, SUBLANES, qb), F32))
        tot = jnp.sum(jnp.sum(cnt, axis=0), axis=0, keepdims=True)
        accept = tot >= float(topk)
        cur = jnp.where(accept, cand, cur)
        ccur = jnp.where(accept, tot, ccur)
        active = jnp.max(jnp.where(ccur == float(topk), 0.0, 1.0))
        return b + 1, cur, ccur, active.astype(jnp.int32)

    _, thr, _, _ = lax.while_loop(
        bit_cond, bit_body,
        (jnp.int32(0), jnp.full((SUBLANES, qb), INT_MIN, jnp.int32),
         jnp.full((SUBLANES, qb), 3e38, F32), jnp.int32(1)))
    thr_row = thr[0:1, :]

    ql_all = jnp.concatenate([qlatt_ref[h] for h in range(nh)], axis=1)
    m_scr[...] = jnp.full(m_scr.shape, NEG_BIG, F32)
    l_scr[...] = jnp.zeros(l_scr.shape, F32)
    acc_scr[...] = jnp.zeros(acc_scr.shape, F32)

    def clamp(j):
        return jnp.minimum(j, nkt_max - 1)

    def scores_into(j, dst, mt_dst):
        jc = clamp(j)
        kept = jnp.where(key_scr[jc] >= thr_row, 0.0, NEG_BIG)
        bias = jnp.where(j * kt + kloc <= qpos, kept, NEG_BIG)
        off = (j * kt - t0).astype(F32)
        lg = jnp.dot(ckv_ref[pl.ds(pl.multiple_of(jc * kt, kt), kt), :], ql_all,
                     preferred_element_type=F32)
        for h in range(nh):
            s = lg[:, head(h)] + alibi_scr[h] + bias
            dst[:, head(h)] = s
            mt_dst[h] = jnp.broadcast_to(jnp.max(s, axis=0, keepdims=True) + slopes[h] * off,
                                         (SUBLANES, qb))

    def softmax_pv(j, src, mt_src):
        off = (j * kt - t0).astype(F32)
        for h in range(nh):
            m_prev = m_scr[h]
            m_next = jnp.maximum(m_prev, mt_src[h])
            alpha = jnp.exp2(m_prev - m_next)
            p = jnp.exp2(src[:, head(h)] - (m_next[0:1, :] - slopes[h] * off))
            l_scr[h] = alpha * l_scr[h] + jnp.sum(p, axis=0, keepdims=True)
            m_scr[h] = m_next
            a_scr[h] = alpha
            p_scr[:, head(h)] = p.astype(BF16)
        pvt = jnp.dot(ckvt_ref[clamp(j)], p_scr[...], preferred_element_type=F32)
        a_row = jnp.concatenate([a_scr[h][0:1, :] for h in range(nh)], axis=1)
        acc_scr[...] = acc_scr[...] * a_row + pvt

    def pair_body(pp, carry):
        j = 2 * pp
        scores_into(j + 1, sb_scr, mtb_scr)
        softmax_pv(j, sa_scr, mta_scr)
        scores_into(j + 2, sa_scr, mta_scr)
        softmax_pv(j + 1, sb_scr, mtb_scr)
        return carry

    scores_into(0, sa_scr, mta_scr)
    lax.fori_loop(0, (nkt + 1) // 2, pair_body, 0)

    for h in range(nh):
        o_latt = (acc_scr[:, head(h)] / l_scr[h][0:1, :]).astype(BF16)
        ot = jnp.dot(wuvt_ref[h], o_latt, preferred_element_type=F32)
        g = gate_ref[:, h * HEAD_DIM:(h + 1) * HEAD_DIM]
        o_ref[:, h * HEAD_DIM:(h + 1) * HEAD_DIM] = (ot.T * _silu(g)).astype(o_ref.dtype)


def _dsa_attention(qidxt, widxt, kidx, qlatt, ckv, ckvt, wuvt, gate, kt, qb=LANES):
    nh, _, s = qidxt.shape
    topk = min(TOPK_MAX, s // 4)
    nkt_max = s // kt
    assert ckvt.shape == (nkt_max, KV_RANK, kt) and kt % SUBLANES == 0
    slopes = tuple(float(v) * math.log2(math.e) for v in _alibi_slopes())
    kern = functools.partial(_dsa_attn_kernel, qb=qb, kt=kt, nkt_max=nkt_max, topk=topk,
                             slopes=slopes)
    stat = pltpu.VMEM((nh, SUBLANES, qb), F32)
    return pl.pallas_call(
        kern,
        grid=(s // qb,),
        in_specs=[
            pl.BlockSpec((nh, IDX_DIM, qb), lambda i: (0, 0, i)),
            pl.BlockSpec((LANES, qb), lambda i: (0, i)),
            pl.BlockSpec((s, IDX_DIM), lambda i: (0, 0)),
            pl.BlockSpec((nh, KV_RANK, qb), lambda i: (0, 0, i)),
            pl.BlockSpec((s, KV_RANK), lambda i: (0, 0)),
            pl.BlockSpec((nkt_max, KV_RANK, kt), lambda i: (0, 0, 0)),
            pl.BlockSpec((nh, HEAD_DIM, KV_RANK), lambda i: (0, 0, 0)),
            pl.BlockSpec((qb, BRANCH), lambda i: (i, 0)),
        ],
        out_specs=pl.BlockSpec((qb, BRANCH), lambda i: (i, 0)),
        out_shape=jax.ShapeDtypeStruct((s, BRANCH), BF16),
        scratch_shapes=[
            pltpu.VMEM((nkt_max, kt, qb), jnp.int32),
            pltpu.VMEM((nh, kt, qb), F32),
            pltpu.VMEM((kt, nh * qb), F32),
            pltpu.VMEM((kt, nh * qb), F32),
            stat,
            stat,
            pltpu.VMEM((kt, nh * qb), BF16),
            stat,
            stat,
            stat,
            pltpu.VMEM((KV_RANK, nh * qb), F32),
        ],
        compiler_params=_cparams(("arbitrary",), V7X_VMEM_LIMIT_BYTES),
        name="dsa_attention",
    )(qidxt, widxt, kidx, qlatt, ckv, ckvt, wuvt, gate)


MOBA_POS_SPLIT = 64
MOBA_SEL_LANE0 = 8
MOBA_MASK = -(2.0 ** 100)


def _moba_key_consts(s):
    pos = np.arange(s)
    kc = np.zeros((s, LANES), np.float32)
    kc[:, 0:3] = (MOBA_POS_SPLIT * (pos // MOBA_POS_SPLIT))[:, None]
    kc[:, 3:6] = (pos % MOBA_POS_SPLIT)[:, None]
    kc[pos, MOBA_SEL_LANE0 + pos // MOBA_BLOCK] = 1.0
    return jnp.asarray(kc, BF16)


def _moba_query_consts():
    bf = lambda a: a.astype(ml_dtypes.bfloat16).astype(np.float32)
    v = (_alibi_slopes() * np.float32(math.log2(math.e))).astype(np.float32)
    hi = bf(v)
    mid = bf(v - hi)
    lo = bf(v - hi - mid)
    qc = np.zeros((N_HEADS, 1, LANES), np.float32)
    for p, piece in enumerate((hi, mid, lo)):
        qc[:, 0, p] = piece
        qc[:, 0, 3 + p] = piece
    return jnp.asarray(qc)


def _moba_attn_kernel(qc_ref, q_ref, k_ref, v_ref, kc_ref, gate_ref, o_ref,
                      kaug_scr, kmean_scr, sa_scr, sb_scr, mta_scr, mtb_scr, m_scr, l_scr, acc_scr,
                      *, nb, qt, kt):
    blk = MOBA_BLOCK
    i = pl.program_id(1)
    sub = qt // blk

    @pl.when(i == 0)
    def _():
        kaug_scr[:, 0:HEAD_DIM] = k_ref[...]
        kaug_scr[:, HEAD_DIM:] = kc_ref[...]
        kmean_scr[...] = jnp.zeros(kmean_scr.shape, F32)
        for j in range(nb):
            kj = k_ref[j * blk:(j + 1) * blk, :].astype(F32)
            r = MOBA_SEL_LANE0 + j
            kmean_scr[r:r + 1, :] = jnp.sum(kj, axis=0, keepdims=True) * (1.0 / blk)

    q = q_ref[...]
    km = kmean_scr[...]
    km_hi = km.astype(BF16)
    km_lo = (km - km_hi.astype(F32)).astype(BF16)
    g = _nt_dot(q, km_hi) + _nt_dot(q, km_lo)
    lane = lax.broadcasted_iota(jnp.int32, (qt, LANES), 1)
    blk_id = lane - MOBA_SEL_LANE0
    own = i * sub + lax.broadcasted_iota(jnp.int32, (qt, LANES), 0) // blk
    past = jnp.logical_and(blk_id >= 0, blk_id < own)
    g = jnp.where(past, g, -jnp.inf)
    selb = jnp.full((qt, LANES), MOBA_MASK, F32)
    for _ in range(MOBA_TOPK):
        pick = lane == jnp.argmax(g, axis=1, keepdims=True).astype(jnp.int32)
        selb = jnp.where(pick, 0.0, selb)
        g = jnp.where(pick, -jnp.inf, g)
    selb = jnp.where(past, selb, MOBA_MASK)
    selb = jnp.where(blk_id == own, 0.0, selb)
    aug = jnp.where(lane < MOBA_SEL_LANE0, qc_ref[...], selb)
    q_aug = jnp.concatenate([q, aug.astype(BF16)], axis=1)

    m_scr[...] = jnp.full(m_scr.shape, MOBA_MASK, F32)
    l_scr[...] = jnp.zeros(l_scr.shape, F32)
    acc_scr[...] = jnp.zeros(acc_scr.shape, F32)

    def scores_into(j, dst, mt_dst, causal):
        c0 = pl.multiple_of(j * kt, kt)
        s = _nt_dot(q_aug, kaug_scr[pl.ds(c0, kt), :])
        if causal:
            qpos = i * qt + lax.broadcasted_iota(jnp.int32, (qt, kt), 0)
            kpos = c0 + lax.broadcasted_iota(jnp.int32, (qt, kt), 1)
            s = jnp.where(kpos <= qpos, s, MOBA_MASK)
        dst[...] = s
        mt_dst[...] = jnp.broadcast_to(jnp.max(s, axis=1, keepdims=True), (qt, LANES))

    def softmax_pv(j, src, mt_src):
        c0 = pl.multiple_of(j * kt, kt)
        m_prev = m_scr[...]
        m_next = jnp.maximum(m_prev, mt_src[...])
        alpha = jnp.exp2(m_prev - m_next)
        p = _lanewise(lambda a, m: jnp.exp2(a - m), src[...], m_next)
        l_scr[...] = alpha * l_scr[...] + jnp.sum(p, axis=1, keepdims=True)
        m_scr[...] = m_next
        acc_scr[...] = acc_scr[...] * alpha + jnp.dot(
            p.astype(BF16), v_ref[pl.ds(c0, kt), :], preferred_element_type=F32)

    def pair_body(pp, carry):
        j = 2 * pp
        scores_into(j + 1, sb_scr, mtb_scr, False)
        softmax_pv(j, sa_scr, mta_scr)
        scores_into(j + 2, sa_scr, mta_scr, True)
        softmax_pv(j + 1, sb_scr, mtb_scr)
        return carry

    scores_into(0, sa_scr, mta_scr, True)
    lax.fori_loop(0, i, pair_body, 0)
    scores_into(2 * i + 1, sb_scr, mtb_scr, True)
    softmax_pv(2 * i, sa_scr, mta_scr)
    softmax_pv(2 * i + 1, sb_scr, mtb_scr)

    o = acc_scr[...] / l_scr[...]
    o_ref[...] = (o * _silu(gate_ref[...])).astype(o_ref.dtype)


def _moba_attention(qkv, gate, qt=1024, kt=512):
    s = qkv.shape[0]
    nb = s // MOBA_BLOCK
    qt = min(qt, s)
    kt = min(kt, qt)
    assert MOBA_SEL_LANE0 + nb <= LANES and s < MOBA_POS_SPLIT * 256
    assert s % qt == 0 and qt == 2 * kt and kt % MOBA_BLOCK == 0
    kern = functools.partial(_moba_attn_kernel, nb=nb, qt=qt, kt=kt)
    return pl.pallas_call(
        kern,
        grid=(N_HEADS, s // qt),
        in_specs=[
            pl.BlockSpec((None, 1, LANES), lambda h, i: (h, 0, 0)),
            pl.BlockSpec((qt, HEAD_DIM), lambda h, i: (i, h)),
            pl.BlockSpec((s, HEAD_DIM), lambda h, i: (0, N_HEADS + h)),
            pl.BlockSpec((s, HEAD_DIM), lambda h, i: (0, 2 * N_HEADS + h)),
            pl.BlockSpec((s, LANES), lambda h, i: (0, 0)),
            pl.BlockSpec((qt, HEAD_DIM), lambda h, i: (i, h)),
        ],
        out_specs=pl.BlockSpec((qt, HEAD_DIM), lambda h, i: (i, h)),
        out_shape=jax.ShapeDtypeStruct((s, BRANCH), BF16),
        scratch_shapes=[
            pltpu.VMEM((s, 2 * HEAD_DIM), BF16),
            pltpu.VMEM((LANES, HEAD_DIM), F32),
            pltpu.VMEM((qt, kt), F32),
            pltpu.VMEM((qt, kt), F32),
            pltpu.VMEM((qt, LANES), F32),
            pltpu.VMEM((qt, LANES), F32),
            pltpu.VMEM((qt, LANES), F32),
            pltpu.VMEM((qt, LANES), F32),
            pltpu.VMEM((qt, HEAD_DIM), F32),
        ],
        compiler_params=_cparams(("arbitrary", "arbitrary"), V7X_VMEM_LIMIT_BYTES),
        name="moba_attention",
    )(_moba_query_consts(), qkv, qkv, qkv, _moba_key_consts(s), gate)


def _out_ln_kernel(og_ref, wo_ref, x_ref, gmod_ref, lng_ref, lnb_ref, scl_ref, shift_ref,
                   xo_ref, ho_ref, *, alpha):
    y = jnp.dot(og_ref[...], wo_ref[...], preferred_element_type=F32)
    z = alpha * x_ref[...] + gmod_ref[...] * y
    mu = jnp.mean(z, axis=-1, keepdims=True)
    zc = z - mu
    var = jnp.mean(zc * zc, axis=-1, keepdims=True)
    xn = zc * lax.rsqrt(var + LN_EPS) * lng_ref[...] + lnb_ref[...]
    xo_ref[...] = xn
    ho_ref[...] = (xn * (1.0 + scl_ref[...]) + shift_ref[...]).astype(ho_ref.dtype)


def _out_proj_ln(og, wo, x, gmod, lng, lnb, scl_next, shift_next, alpha, tm=512):
    s, d = x.shape
    vec = pl.BlockSpec((1, d), lambda i: (0, 0))
    row = pl.BlockSpec((tm, d), lambda i: (i, 0))
    kern = functools.partial(_out_ln_kernel, alpha=alpha)
    return pl.pallas_call(
        kern,
        grid=(s // tm,),
        in_specs=[row, pl.BlockSpec((d, d), lambda i: (0, 0)), row, vec, vec, vec, vec, vec],
        out_specs=[row, row],
        out_shape=[jax.ShapeDtypeStruct((s, d), F32), jax.ShapeDtypeStruct((s, d), BF16)],
        compiler_params=_cparams(("arbitrary",), V7X_VMEM_LIMIT_BYTES),
        name="out_proj_ln",
    )(og, wo, x, gmod, lng, lnb, scl_next, shift_next)


def kernel(x, c, ada_w, ada_b, ln_g, ln_b, dsa_w_in, dsa_g_q, dsa_g_kv, dsa_w_uq, dsa_w_qi,
           dsa_w_uk, dsa_w_uv, dsa_w_o, moba_w_in, moba_w_o):
    batch, s, d = x.shape
    assert batch == 1 and d == BRANCH
    depth = ada_w.shape[0]
    alpha = float((2 * depth) ** 0.25)
    scale = HEAD_DIM ** -0.5

    mod = _adaln_mod(c, ada_w, ada_b)
    shift = lambda i: mod[i, :, 0:d]
    scl = lambda i: mod[i, :, d:2 * d]
    gmod = lambda i: mod[i, :, 2 * d:3 * d]

    xs = x.reshape(s, d)
    h = _modulate(xs, scl(0), shift(0))

    o0, o1, o2, o3 = Q_RANK, Q_RANK + KV_RANK, Q_RANK + KV_RANK + IDX_DIM, \
        Q_RANK + KV_RANK + IDX_DIM + IDX_HEADS
    for i in range(depth):
        j = i // 2
        if i % 2 == 0:
            w_in = dsa_w_in[j]
            wq = w_in[:, :o0].astype(BF16)
            wkv = w_in[:, o0:o1].astype(BF16)
            wki = w_in[:, o1:o2].astype(BF16)
            wwi = jnp.pad(w_in[:, o2:o3] * (IDX_HEADS ** -0.5),
                          ((0, 0), (0, LANES - IDX_HEADS))).astype(BF16)
            wg = w_in[:, o3:].astype(BF16)
            wqit = (dsa_w_qi[j] * (IDX_DIM ** -0.5)).T.astype(BF16)
            wuqt = dsa_w_uq[j].T.astype(BF16)
            wukt_h = (dsa_w_uk[j] * (scale * math.log2(math.e))).transpose(0, 2, 1).astype(BF16)
            wuvt_h = dsa_w_uv[j].transpose(0, 2, 1).astype(BF16)
            wo = dsa_w_o[j].astype(BF16)

            kt = min(DSA_KEY_TILE, s)
            cq, ckv, ckvt, kidx, widxt = _dsa_latent(
                h, wq, wkv, wki, wwi, dsa_g_q[j].reshape(1, -1), dsa_g_kv[j].reshape(1, -1), tm=kt)
            gate = _matmul(h, wg, F32, name="dsa_gate")
            qidxt, qlatt = _dsa_query(cq, wqit, wuqt, wukt_h)
            og = _dsa_attention(qidxt, widxt, kidx, qlatt, ckv, ckvt, wuvt_h, gate, kt=kt)
        else:
            wo = moba_w_o[j].astype(BF16)
            qkv = _proj_stacked(h, moba_w_in, j, 0, 3 * BRANCH, BF16,
                                scale=scale * math.log2(math.e), scaled_cols=BRANCH,
                                name="moba_qkv")
            gate = _proj_stacked(h, moba_w_in, j, 3 * BRANCH, BRANCH, F32, name="moba_gate")
            og = _moba_attention(qkv, gate)
        nxt = min(i + 1, depth - 1)
        xs, h = _out_proj_ln(og, wo, xs, gmod(i), ln_g[i].reshape(1, d), ln_b[i].reshape(1, d),
                             scl(nxt), shift(nxt), alpha)
    return xs.reshape(batch, s, d)
```
